```python
import math
import jax, jax.numpy as jnp
from jax import lax
import numpy as np

D_MODEL = 1024
BATCH = 8
SEQ = 2048
DEPTH = 1

GRID_W = 64
CTX_LEN = 256
NA_HEADS = 8
NA_HEAD_DIM = 64
NA_KH = 8
NA_KW = 16
NA_SCALE = NA_HEAD_DIM ** -0.5
DN_HEADS = 8
DN_HEAD_DIM = 64
DN_CONV = 5
DN_CHUNK = 64
ROPE_THETA = 10000.0
N_EXPERTS = 32
TOP_K = 4
D_EXPERT = 1024
SWIGLU_LIMIT = 7.0
SWIGLU_ALPHA = 1.702
MOE_BLOCK = 128
NORM_EPS = 1e-6
NEG_INF = -1e30
NA_WIDTH = NA_HEADS * NA_HEAD_DIM
DN_WIDTH = DN_HEADS * DN_HEAD_DIM
IN_SPLITS = (NA_WIDTH, NA_WIDTH, NA_WIDTH, 3 * DN_WIDTH, DN_WIDTH, 2 * DN_HEADS, 2 * DN_HEADS, D_MODEL, D_MODEL)
D_IN = sum(IN_SPLITS)
IN_OFFSETS = tuple(int(o) for o in np.cumsum(IN_SPLITS)[:-1])

kernel_name = 'hybrid_na_gdn_moe_dit_block'


def rmsnorm(x, w):
    xf = x.astype(jnp.float32)
    y = xf * lax.rsqrt(jnp.mean(xf * xf, axis=-1, keepdims=True) + NORM_EPS)
    return (y * w.astype(jnp.float32)).astype(x.dtype)


def modulate(h, shift, scale):
    return h * (1.0 + scale) + shift


def to_heads(t, heads):
    b, t_len, _ = t.shape
    return t.reshape(b, t_len, heads, -1).transpose(0, 2, 1, 3)


def merge_heads(o):
    b, h, t_len, d = o.shape
    return o.transpose(0, 2, 1, 3).reshape(b, t_len, h * d)


def l2norm(x):
    return x * lax.rsqrt(jnp.sum(x * x, axis=-1, keepdims=True) + NORM_EPS)


def centred_dwconv(x, w):
    taps, ch = w.shape
    return lax.conv_general_dilated(x, w[:, None, :].astype(x.dtype), window_strides=(1,),
                                    padding=[(taps // 2, taps // 2)],
                                    dimension_numbers=('NWC', 'WIO', 'NWC'), feature_group_count=ch)


def axial_rope(x, t_len):
    t = jnp.arange(t_len)
    row = (t // GRID_W).astype(jnp.float32)
    col = (t % GRID_W).astype(jnp.float32)
    half = x.shape[-1] // 2
    n_axis = half // 2
    freqs = ROPE_THETA ** (-jnp.arange(n_axis, dtype=jnp.float32) / n_axis)
    ang = jnp.concatenate([row[:, None] * freqs, col[:, None] * freqs], axis=-1)
    cos, sin = jnp.cos(ang), jnp.sin(ang)
    x1, x2 = x[..., :half], x[..., half:]
    return jnp.concatenate([x1 * cos - x2 * sin, x1 * sin + x2 * cos], axis=-1)


def neighborhood_attention(q, k, v, k_ctx, v_ctx, rpb, rows):
    b, h, t_len, dh = q.shape
    kh = min(NA_KH, rows)
    r = jnp.arange(rows)
    row_idx = jnp.clip(r - kh // 2, 0, rows - kh)[:, None] + jnp.arange(kh)[None, :]
    col = jnp.arange(GRID_W)
    col_start = jnp.clip(col - NA_KW // 2, 0, GRID_W - NA_KW)
    col_mask = (col[None, :] >= col_start[:, None]) & (col[None, :] < col_start[:, None] + NA_KW)
    dr_idx = row_idx - r[:, None] + NA_KH - 1
    dc_idx = jnp.clip(col[None, :] - col[:, None] + NA_KW - 1, 0, 2 * NA_KW - 2)
    bias = rpb[:, dr_idx[:, None, :, None], dc_idx[None, :, None, :]].astype(jnp.float32)
    qg = q.reshape(b, h, rows, GRID_W, dh)
    kg = k.reshape(b, h, rows, GRID_W, dh)[:, :, row_idx]
    vg = v.reshape(b, h, rows, GRID_W, dh)[:, :, row_idx]
    s_lat = jnp.einsum('bhrqd,bhrjkd->bhrqjk', qg, kg).astype(jnp.float32) * NA_SCALE + bias
    s_lat = jnp.where(col_mask[:, None, :], s_lat, NEG_INF)
    s_ctx = jnp.einsum('bhrqd,bhld->bhrql', qg, k_ctx).astype(jnp.float32) * NA_SCALE
    n_lat = kh * GRID_W
    s = jnp.concatenate([s_lat.reshape(b, h, rows, GRID_W, n_lat), s_ctx], axis=-1)
    p = jax.nn.softmax(s, axis=-1).astype(v.dtype)
    p_lat = p[..., :n_lat].reshape(b, h, rows, GRID_W, kh, GRID_W)
    o = (jnp.einsum('bhrqjk,bhrjkd->bhrqd', p_lat, vg)
         + jnp.einsum('bhrql,bhld->bhrqd', p[..., n_lat:], v_ctx))
    return merge_heads(o.reshape(b, h, t_len, dh))


def context_attention(q, k, v):
    s = jnp.einsum('bhqd,bhkd->bhqk', q, k).astype(jnp.float32) * NA_SCALE
    p = jax.nn.softmax(s, axis=-1).astype(v.dtype)
    return merge_heads(jnp.einsum('bhqk,bhkd->bhqd', p, v))


def gated_delta_chunked(q, k, v, g, beta, s0):
    b, h, t_len, dk = q.shape
    dv = v.shape[-1]
    n = t_len // DN_CHUNK
    chunk = lambda t: t.reshape(b, h, n, DN_CHUNK, *t.shape[3:])
    qc, kc, vc, bc = chunk(q * dk ** -0.5), chunk(k), chunk(v), chunk(beta)
    gcum = jnp.cumsum(chunk(g), axis=-1)
    incl = jnp.tril(jnp.ones((DN_CHUNK, DN_CHUNK), dtype=bool))
    strict = jnp.tril(jnp.ones((DN_CHUNK, DN_CHUNK), dtype=bool), -1)
    diff = gcum[..., :, None] - gcum[..., None, :]
    decay = jnp.where(incl, jnp.exp(jnp.where(incl, diff, 0.0)), 0.0)
    kb = kc * bc[..., None]
    a = jnp.where(strict, jnp.einsum('bhnid,bhnjd->bhnij', kb, kc) * decay, 0.0)
    eye = jnp.eye(DN_CHUNK, dtype=a.dtype)
    t_inv = lax.linalg.triangular_solve(a + eye, jnp.broadcast_to(eye, a.shape), left_side=True,
                                        lower=True, unit_diagonal=True)
    u = jnp.einsum('bhnij,bhnjd->bhnid', t_inv, vc * bc[..., None])
    w = jnp.einsum('bhnij,bhnjd->bhnid', t_inv, kb * jnp.exp(gcum)[..., None])
    qk = jnp.einsum('bhnid,bhnjd->bhnij', qc, kc) * decay
    q_dec = qc * jnp.exp(gcum)[..., None]
    g_last = gcum[..., -1]
    k_tail = kc * jnp.exp(g_last[..., None] - gcum)[..., None]

    def step(s, xs):
        u_n, w_n, qk_n, qd_n, kt_n, gl_n = xs
        v_new = u_n - jnp.einsum('bhid,bhde->bhie', w_n, s)
        o_n = jnp.einsum('bhid,bhde->bhie', qd_n, s) + jnp.einsum('bhij,bhje->bhie', qk_n, v_new)
        s = s * jnp.exp(gl_n)[..., None, None] + jnp.einsum('bhid,bhie->bhde', kt_n, v_new)
        return s, o_n

    xs = tuple(jnp.moveaxis(t, 2, 0) for t in (u, w, qk, q_dec, k_tail, g_last))
    s_fin, o = lax.scan(step, s0, xs)
    return s_fin, jnp.moveaxis(o, 0, 2).reshape(b, h, t_len, dv)


def dn_prepare(z_qkv, z_beta, z_a, conv_w, a_log, dt_bias, use_rope):
    qkv = jax.nn.silu(centred_dwconv(z_qkv, conv_w)).astype(jnp.float32)
    q, k, v = jnp.split(qkv, 3, axis=-1)
    q = l2norm(to_heads(q, DN_HEADS))
    k = l2norm(to_heads(k, DN_HEADS))
    v = to_heads(v, DN_HEADS)
    if use_rope:
        q = axial_rope(q, q.shape[2])
        k = axial_rope(k, k.shape[2])
    b, t_len, _ = z_beta.shape
    beta = jax.nn.sigmoid(z_beta.astype(jnp.float32)).reshape(b, t_len, 2, DN_HEADS).transpose(2, 0, 3, 1)
    a = z_a.astype(jnp.float32).reshape(b, t_len, 2, DN_HEADS).transpose(2, 0, 3, 1)
    g = -jnp.exp(a_log.astype(jnp.float32))[:, None, :, None] * jax.nn.softplus(
        a + dt_bias.astype(jnp.float32)[:, None, :, None])
    return q, k, v, g, beta


def dn_bidirectional(ctx_in, lat_in):
    qc, kc, vc, gc, bc = ctx_in
    ql, kl, vl, gl, bl = lat_in
    b, h, _, dk = kc.shape
    s0 = jnp.zeros((b, h, dk, vc.shape[-1]), jnp.float32)
    flip = lambda t: jnp.flip(t, axis=2)
    s_cf, o_cf = gated_delta_chunked(qc, kc, vc, gc[0], bc[0], s0)
    _, o_lf = gated_delta_chunked(ql, kl, vl, gl[0], bl[0], s_cf)
    s_cb, o_cb = gated_delta_chunked(flip(qc), flip(kc), flip(vc), flip(gc[1]), flip(bc[1]), s0)
    _, o_lb = gated_delta_chunked(flip(ql), flip(kl), flip(vl), flip(gl[1]), flip(bl[1]), s_cb)
    return o_cf + flip(o_cb), o_lf + flip(o_lb)


def dn_output(o, z_g, norm_w):
    b, h, t_len, dv = o.shape
    o = rmsnorm(o.transpose(0, 2, 1, 3), norm_w) * jax.nn.silu(z_g.astype(jnp.float32).reshape(b, t_len, h, dv))
    return o.reshape(b, t_len, h * dv).astype(z_g.dtype)


def merge_branches(o_na, o_dn, ga, gb, w_br_a, w_br_b, w_out):
    y = jax.nn.sigmoid(ga) * (o_na @ w_br_a) + jax.nn.sigmoid(gb) * (o_dn @ w_br_b)
    return y @ w_out


def mixer_sublayer(h_lat, h_ctx, w_in, na_rpb, dn_conv_w, dn_a_log, dn_dt_bias, dn_norm_w,
                   w_br_a, w_br_b, w_out, need_ctx_out):
    rows = h_lat.shape[1] // GRID_W
    naq_l, nak_l, nav_l, dqkv_l, dg_l, db_l, da_l, ga_l, gb_l = jnp.split(h_lat @ w_in, IN_OFFSETS, axis=-1)
    naq_c, nak_c, nav_c, dqkv_c, dg_c, db_c, da_c, ga_c, gb_c = jnp.split(h_ctx @ w_in, IN_OFFSETS, axis=-1)
    k_ctx = to_heads(nak_c, NA_HEADS)
    v_ctx = to_heads(nav_c, NA_HEADS)
    o_na_l = neighborhood_attention(to_heads(naq_l, NA_HEADS), to_heads(nak_l, NA_HEADS),
                                    to_heads(nav_l, NA_HEADS), k_ctx, v_ctx, na_rpb, rows)
    dn_c = dn_prepare(dqkv_c, db_c, da_c, dn_conv_w, dn_a_log, dn_dt_bias, False)
    dn_l = dn_prepare(dqkv_l, db_l, da_l, dn_conv_w, dn_a_log, dn_dt_bias, True)
    o_dn_c, o_dn_l = dn_bidirectional(dn_c, dn_l)
    y_lat = merge_branches(o_na_l, dn_output(o_dn_l, dg_l, dn_norm_w), ga_l, gb_l, w_br_a, w_br_b, w_out)
    y_ctx = None
    if need_ctx_out:
        o_na_c = context_attention(to_heads(naq_c, NA_HEADS), k_ctx, v_ctx)
        y_ctx = merge_branches(o_na_c, dn_output(o_dn_c, dg_c, dn_norm_w), ga_c, gb_c, w_br_a, w_br_b, w_out)
    return y_lat, y_ctx


def moe_ffn(h, w_router, b_router, w1, b1, w2, b2):
    b, t_len, d = h.shape
    xf = h.reshape(-1, d)
    n_tok = xf.shape[0]
    logits = (xf @ w_router + b_router).astype(jnp.float32)
    top_val, top_idx = lax.top_k(logits, TOP_K)
    gates = jax.nn.softmax(top_val, axis=-1)
    n_asg = n_tok * TOP_K
    e_flat = top_idx.reshape(-1)
    tok_flat = jnp.repeat(jnp.arange(n_tok, dtype=jnp.int32), TOP_K)
    order = jnp.argsort(e_flat, stable=True)
    e_sorted, tok_sorted, g_sorted = e_flat[order], tok_flat[order], gates.reshape(-1)[order]
    counts = jnp.zeros((N_EXPERTS,), jnp.int32).at[e_flat].add(1)
    starts = jnp.cumsum(counts) - counts
    padded = (counts + MOE_BLOCK - 1) // MOE_BLOCK * MOE_BLOCK
    pad_ends = jnp.cumsum(padded)
    pad_starts = pad_ends - padded
    dest = pad_starts[e_sorted] + (jnp.arange(n_asg, dtype=jnp.int32) - starts[e_sorted])
    m_pad = (n_asg + N_EXPERTS * (MOE_BLOCK - 1) + MOE_BLOCK - 1) // MOE_BLOCK * MOE_BLOCK
    n_blk = m_pad // MOE_BLOCK
    tok_pad = jnp.full((m_pad,), n_tok, jnp.int32).at[dest].set(tok_sorted)
    g_pad = jnp.zeros((m_pad,), jnp.float32).at[dest].set(g_sorted)
    blk_expert = jnp.minimum(jnp.searchsorted(pad_ends, jnp.arange(n_blk, dtype=jnp.int32) * MOE_BLOCK,
                                              side='right'), N_EXPERTS - 1)
    x_pad = jnp.concatenate([xf, jnp.zeros((1, d), xf.dtype)], axis=0)[tok_pad].reshape(n_blk, MOE_BLOCK, d)

    def expert_block(args):
        xb, e = args
        hb = xb @ w1[e] + b1[e]
        gate, up = hb[:, :D_EXPERT], hb[:, D_EXPERT:]
        gate = jnp.minimum(gate, SWIGLU_LIMIT)
        up = jnp.clip(up, -SWIGLU_LIMIT, SWIGLU_LIMIT)
        act = (up + 1.0) * gate * jax.nn.sigmoid(SWIGLU_ALPHA * gate)
        return act @ w2[e] + b2[e]

    y_pad = lax.map(expert_block, (x_pad, blk_expert)).reshape(m_pad, d)
    y = jax.ops.segment_sum(y_pad.astype(jnp.float32) * g_pad[:, None], tok_pad, num_segments=n_tok + 1)[:n_tok]
    return y.reshape(b, t_len, d).astype(h.dtype)


def setup_inputs(seed: int = 0) -> dict:
    key = jax.random.key(seed)
    ks = jax.random.split(key, 24)
    f32 = jnp.float32
    L, D, E, F = DEPTH, D_MODEL, N_EXPERTS, D_EXPERT

    def nrm(k, shape, scale):
        return jax.random.normal(k, shape, f32) * scale

    dt = jnp.exp(jax.random.uniform(ks[10], (L, 2, DN_HEADS), f32, math.log(1e-3), math.log(1e-1)))
    return {
        'x': nrm(ks[0], (BATCH, SEQ, D), 1.0),
        'c': nrm(ks[1], (BATCH, D), 1.0),
        'ctx': nrm(ks[2], (BATCH, CTX_LEN, D), 1.0),
        'c_ctx': nrm(ks[3], (D,), 1.0),
        'w_mod': nrm(ks[4], (L, D, 6 * D), 0.5 * D ** -0.5),
        'b_mod': nrm(ks[5], (L, 6 * D), 0.01),
        'norm1_w': 1.0 + nrm(ks[6], (L, D), 0.01),
        'w_in': nrm(ks[7], (L, D, D_IN), D ** -0.5),
        'na_rpb': nrm(ks[8], (L, NA_HEADS, 2 * NA_KH - 1, 2 * NA_KW - 1), 0.1),
        'dn_conv_w': nrm(ks[9], (L, DN_CONV, 3 * DN_WIDTH), DN_CONV ** -0.5),
        'dn_a_log': jnp.log(jax.random.uniform(ks[11], (L, 2, DN_HEADS), f32, 1.0, 16.0)),
        'dn_dt_bias': dt + jnp.log(-jnp.expm1(-dt)),
        'dn_norm_w': 1.0 + nrm(ks[12], (L, DN_HEAD_DIM), 0.01),
        'w_br_a': nrm(ks[13], (L, NA_WIDTH, D), NA_WIDTH ** -0.5),
        'w_br_b': nrm(ks[14], (L, DN_WIDTH, D), DN_WIDTH ** -0.5),
        'w_out': nrm(ks[15], (L, D, D), D ** -0.5),
        'norm2_w': 1.0 + nrm(ks[16], (L, D), 0.01),
        'w_router': nrm(ks[17], (L, D, E), D ** -0.5),
        'b_router': nrm(ks[18], (L, E), 0.01),
        'w1': nrm(ks[19], (L, E, D, 2 * F), D ** -0.5),
        'b1': nrm(ks[20], (L, E, 2 * F), 0.01),
        'w2': nrm(ks[21], (L, E, F, D), F ** -0.5),
        'b2': nrm(ks[22], (L, E, D), 0.01),
        'final_norm_w': 1.0 + nrm(ks[23], (D,), 0.01),
    }


def reference(x, c, ctx, c_ctx, w_mod, b_mod, norm1_w, w_in, na_rpb, dn_conv_w, dn_a_log, dn_dt_bias,
              dn_norm_w, w_br_a, w_br_b, w_out, norm2_w, w_router, b_router, w1, b1, w2, b2, final_norm_w):
    xl, xc = x, ctx
    for layer in range(DEPTH):
        last = layer == DEPTH - 1
        mod_l = (jax.nn.silu(c) @ w_mod[layer] + b_mod[layer])[:, None, :]
        mod_c = (jax.nn.silu(c_ctx) @ w_mod[layer] + b_mod[layer])[None, None, :]
        sh1_l, sc1_l, g1_l, sh2_l, sc2_l, g2_l = jnp.split(mod_l, 6, axis=-1)
        sh1_c, sc1_c, g1_c, sh2_c, sc2_c, g2_c = jnp.split(mod_c, 6, axis=-1)
        hl = modulate(rmsnorm(xl, norm1_w[layer]), sh1_l, sc1_l)
        hc = modulate(rmsnorm(xc, norm1_w[layer]), sh1_c, sc1_c)
        yl, yc = mixer_sublayer(hl, hc, w_in[layer], na_rpb[layer], dn_conv_w[layer], dn_a_log[layer],
                                dn_dt_bias[layer], dn_norm_w[layer], w_br_a[layer], w_br_b[layer],
                                w_out[layer], not last)
        xl = xl + g1_l * yl
        xl = xl + g2_l * moe_ffn(modulate(rmsnorm(xl, norm2_w[layer]), sh2_l, sc2_l), w_router[layer],
                                 b_router[layer], w1[layer], b1[layer], w2[layer], b2[layer])
        if not last:
            xc = xc + g1_c * yc
            xc = xc + g2_c * moe_ffn(modulate(rmsnorm(xc, norm2_w[layer]), sh2_c, sc2_c), w_router[layer],
                                     b_router[layer], w1[layer], b1[layer], w2[layer], b2[layer])
    return rmsnorm(xl, final_norm_w)
```

```python
import functools

import numpy as np
import jax
import jax.numpy as jnp
from jax import lax
from jax.experimental import pallas as pl
from jax.experimental.pallas import tpu as pltpu

F32 = jnp.float32
BF16 = jnp.bfloat16
HIGHEST = lax.Precision.HIGHEST

D_MODEL = 1024
BATCH = 8
SEQ = 2048
GRID_W = 64
GRID_ROWS = SEQ // GRID_W
CTX_LEN = 256
NA_HEADS = 8
NA_HEAD_DIM = 64
NA_KH = 8
NA_KW = 16
NA_SCALE = NA_HEAD_DIM ** -0.5
NA_WIDTH = NA_HEADS * NA_HEAD_DIM
DN_HEADS = 8
DN_HEAD_DIM = 64
DN_WIDTH = DN_HEADS * DN_HEAD_DIM
DN_CONV = 5
DN_CHUNK = 64
ROPE_THETA = 10000.0
N_EXPERTS = 32
TOP_K = 4
D_EXPERT = 1024
SWIGLU_LIMIT = 7.0
SWIGLU_ALPHA = 1.702
NORM_EPS = 1e-6
NEG_INF = -1e30

LANES = 128
N_TOK = BATCH * SEQ
N_ASG = N_TOK * TOP_K
MOE_BM = 256
MOE_MPAD = -(-(N_ASG + N_EXPERTS * (MOE_BM - 1)) // MOE_BM) * MOE_BM
MOE_NBLK = MOE_MPAD // MOE_BM
VMEM_LIMIT = 56 * 1024 * 1024


def _params(sem, vmem=None):
    return pltpu.CompilerParams(dimension_semantics=sem, vmem_limit_bytes=vmem)


def _mod_kernel(c_ref, w_ref, b_ref, o_ref):
    c = c_ref[...]
    s = c * jax.nn.sigmoid(c)
    o_ref[...] = jnp.dot(s, w_ref[...], precision=HIGHEST, preferred_element_type=F32) + b_ref[...]


def _modulation(cc, w_mod, b_mod):
    rows, d = cc.shape
    n = w_mod.shape[1]
    bn = 1024
    return pl.pallas_call(
        _mod_kernel, grid=(n // bn,),
        in_specs=[pl.BlockSpec((rows, d), lambda j: (0, 0)),
                  pl.BlockSpec((d, bn), lambda j: (0, j)),
                  pl.BlockSpec((1, bn), lambda j: (0, j))],
        out_specs=pl.BlockSpec((rows, bn), lambda j: (0, j)),
        out_shape=jax.ShapeDtypeStruct((rows, n), F32),
        compiler_params=_params(("parallel",)), name="modulation")(cc, w_mod, b_mod)


def _inproj_kernel(x_ref, sh_ref, sc_ref, nw_ref, wna_ref, wdq_ref, wdg_ref, wgab_ref, wdba_ref,
                   q_ref, k_ref, v_ref, dq_ref, dg_ref, gab_ref, dba_ref):
    x = x_ref[0]
    ms = jnp.mean(x * x, axis=-1, keepdims=True)
    y = x * lax.rsqrt(ms + NORM_EPS) * nw_ref[...]
    h = (y * (1.0 + sc_ref[0]) + sh_ref[0]).astype(BF16)
    na = jnp.dot(h, wna_ref[...], preferred_element_type=F32)
    q_ref[0] = (na[:, :NA_WIDTH] * NA_SCALE).astype(BF16)
    k_ref[0] = na[:, NA_WIDTH:2 * NA_WIDTH].astype(BF16)
    v_ref[0] = na[:, 2 * NA_WIDTH:].astype(BF16)
    dq_ref[0] = jnp.dot(h, wdq_ref[...], preferred_element_type=F32)
    dg_ref[0] = jnp.dot(h, wdg_ref[...], preferred_element_type=F32)
    gab_ref[0] = jnp.dot(h, wgab_ref[...], preferred_element_type=F32)
    dba_ref[0] = jnp.dot(h, wdba_ref[...], preferred_element_type=F32)


def _in_projection(x, shift, scale, norm_w, weights, tm):
    bx, tx, d = x.shape
    wna, wdq, wdg, wgab, wdba = weights
    tok = lambda w: pl.BlockSpec((1, tm, w), lambda b, i: (b, i, 0))
    row = pl.BlockSpec((1, 1, d), lambda b, i: (b, 0, 0))
    const = lambda a: pl.BlockSpec(a.shape, lambda b, i: (0,) * a.ndim)
    widths = (NA_WIDTH, NA_WIDTH, NA_WIDTH, 3 * DN_WIDTH, DN_WIDTH, 2 * D_MODEL, LANES)
    dtypes = (BF16, BF16, BF16, F32, F32, F32, F32)
    return pl.pallas_call(
        _inproj_kernel, grid=(bx, tx // tm),
        in_specs=[tok(d), row, row, const(norm_w), const(wna), const(wdq), const(wdg), const(wgab), const(wdba)],
        out_specs=[tok(w) for w in widths],
        out_shape=[jax.ShapeDtypeStruct((bx, tx, w), dt) for w, dt in zip(widths, dtypes)],
        compiler_params=_params(("parallel", "parallel"), VMEM_LIMIT), name="in_projection",
    )(x, shift, scale, norm_w, wna, wdq, wdg, wgab, wdba)


def _na_bias_table(rpb):
    col = np.arange(GRID_W)
    col_start = np.clip(col - NA_KW // 2, 0, GRID_W - NA_KW)
    mask = (col[None, :] >= col_start[:, None]) & (col[None, :] < col_start[:, None] + NA_KW)
    dc = np.clip(col[None, :] - col[:, None] + NA_KW - 1, 0, 2 * NA_KW - 2)
    dr = np.arange(NA_KH)[:, None] + np.arange(NA_KH)[None, :]
    tbl = rpb[:, dr[:, :, None, None], dc[None, None, :, :]].astype(F32)
    tbl = jnp.where(mask[None, None, None], tbl, NEG_INF)
    return tbl.transpose(1, 0, 3, 2, 4).reshape(NA_KH, NA_HEADS, GRID_W, NA_KH * GRID_W)


def _na_first_row(r):
    return jnp.clip(r - NA_KH // 2, 0, GRID_ROWS - NA_KH)


def _na_kernel(q_ref, k_ref, v_ref, kc_ref, vc_ref, bias_ref, o_ref):
    r = pl.program_id(1)
    start = pl.multiple_of(_na_first_row(r) * GRID_W, GRID_W)
    q = q_ref[0]
    kw = k_ref[0, pl.ds(start, NA_KH * GRID_W), :]
    vw = v_ref[0, pl.ds(start, NA_KH * GRID_W), :]
    kc = kc_ref[0]
    vc = vc_ref[0]
    nt = (((1,), (1,)), ((), ()))
    outs = []
    for h in range(NA_HEADS):
        sl = slice(h * NA_HEAD_DIM, (h + 1) * NA_HEAD_DIM)
        qh = q[:, sl]
        s = lax.dot_general(qh, kw[:, sl], nt, preferred_element_type=F32) + bias_ref[0, h]
        sc = lax.dot_general(qh, kc[:, sl], nt, preferred_element_type=F32)
        m = jnp.maximum(jnp.max(s, axis=-1, keepdims=True), jnp.max(sc, axis=-1, keepdims=True))
        p = jnp.exp(s - m)
        pc = jnp.exp(sc - m)
        denom = jnp.sum(p, axis=-1, keepdims=True) + jnp.sum(pc, axis=-1, keepdims=True)
        o = (jnp.dot(p.astype(BF16), vw[:, sl], preferred_element_type=F32)
             + jnp.dot(pc.astype(BF16), vc[:, sl], preferred_element_type=F32))
        outs.append(o / denom)
    o_ref[0] = jnp.concatenate(outs, axis=1).astype(BF16)


def _neighborhood_attention(q, k, v, kc, vc, bias):
    b, t, w = q.shape

    def bias_map(bi, r):
        return (_na_first_row(r) - r + NA_KH - 1, 0, 0, 0)

    return pl.pallas_call(
        _na_kernel, grid=(b, GRID_ROWS),
        in_specs=[pl.BlockSpec((1, GRID_W, w), lambda bi, r: (bi, r, 0)),
                  pl.BlockSpec((1, t, w), lambda bi, r: (bi, 0, 0)),
                  pl.BlockSpec((1, t, w), lambda bi, r: (bi, 0, 0)),
                  pl.BlockSpec((1, CTX_LEN, w), lambda bi, r: (bi, 0, 0)),
                  pl.BlockSpec((1, CTX_LEN, w), lambda bi, r: (bi, 0, 0)),
                  pl.BlockSpec((1, NA_HEADS, GRID_W, NA_KH * GRID_W), bias_map)],
        out_specs=pl.BlockSpec((1, GRID_W, w), lambda bi, r: (bi, r, 0)),
        out_shape=jax.ShapeDtypeStruct((b, t, w), BF16),
        compiler_params=_params(("parallel", "parallel"), VMEM_LIMIT), name="neighborhood_attention",
    )(q, k, v, kc, vc, bias)


def _rope_tables(t_len):
    t = jnp.arange(t_len)
    row = (t // GRID_W).astype(F32)
    col = (t % GRID_W).astype(F32)
    n_axis = DN_HEAD_DIM // 4
    freqs = ROPE_THETA ** (-jnp.arange(n_axis, dtype=F32) / n_axis)
    ang = jnp.concatenate([row[:, None] * freqs, col[:, None] * freqs], axis=-1)
    cos = jnp.tile(jnp.cos(ang), (1, 4))
    sin = jnp.sin(ang)
    sin_signed = jnp.tile(jnp.concatenate([-sin, sin], axis=-1), (1, 2))
    return cos, sin_signed


def _dnprep_kernel(z_ref, w_ref, cos_ref, sin_ref, o_ref, *, rope):
    j = pl.program_id(1)
    x = z_ref[0]
    tx = x.shape[0]
    t = lax.broadcasted_iota(jnp.int32, x.shape, 0)
    acc = x * w_ref[DN_CONV // 2:DN_CONV // 2 + 1, :]
    for tap in range(DN_CONV):
        shift = DN_CONV // 2 - tap
        if shift == 0:
            continue
        xs = pltpu.roll(x, shift % tx, axis=0)
        valid = (t >= shift) & (t < tx + shift)
        acc = acc + jnp.where(valid, xs, 0.0) * w_ref[tap:tap + 1, :]
    y = acc * jax.nn.sigmoid(acc)

    @pl.when(j >= 2 * DN_WIDTH // LANES)
    def _():
        o_ref[0] = y

    @pl.when(j < 2 * DN_WIDTH // LANES)
    def _():
        li = lax.broadcasted_iota(jnp.int32, (LANES, LANES), 0) // DN_HEAD_DIM
        lj = lax.broadcasted_iota(jnp.int32, (LANES, LANES), 1) // DN_HEAD_DIM
        same_head = (li == lj).astype(F32)
        ss = jnp.dot(y * y, same_head, precision=HIGHEST, preferred_element_type=F32)
        yn = y * lax.rsqrt(ss + NORM_EPS)
        if rope:
            lane = lax.broadcasted_iota(jnp.int32, x.shape, 1)
            half = DN_HEAD_DIM // 2
            partner = jnp.where(lane % DN_HEAD_DIM < half,
                                pltpu.roll(yn, LANES - half, axis=1), pltpu.roll(yn, half, axis=1))
            yn = yn * cos_ref[...] + partner * sin_ref[...]
        o_ref[0] = jnp.where(j < DN_WIDTH // LANES, yn * DN_HEAD_DIM ** -0.5, yn)


def _dn_prepare(z_qkv, conv_w, cos, sin, rope):
    bx, tx, w = z_qkv.shape
    blk = pl.BlockSpec((1, tx, LANES), lambda b, j: (b, 0, j))
    tbl = pl.BlockSpec((tx, LANES), lambda b, j: (0, 0))
    return pl.pallas_call(
        functools.partial(_dnprep_kernel, rope=rope), grid=(bx, w // LANES),
        in_specs=[blk, pl.BlockSpec((DN_CONV, LANES), lambda b, j: (0, j)), tbl, tbl],
        out_specs=blk, out_shape=jax.ShapeDtypeStruct((bx, tx, w), F32),
        compiler_params=_params(("parallel", "parallel"), VMEM_LIMIT), name="dn_prepare",
    )(z_qkv, conv_w, cos, sin)


def _gate_select_matrices():
    sel = np.zeros((DN_HEADS // 2, LANES, LANES), np.float32)
    for hp in range(DN_HEADS // 2):
        for d in range(2):
            for hh in range(2):
                src = d * DN_HEADS + 2 * hp + hh
                sel[hp, src, 2 * d + hh] = 1.0
                sel[hp, 2 * DN_HEADS + src, 4 + 2 * d + hh] = 1.0
    return jnp.asarray(sel)


def _softplus(x):
    return jnp.maximum(x, 0.0) + jnp.log1p(jnp.exp(-jnp.abs(x)))


def _gates_kernel(z_ref, nega_ref, dtb_ref, sel_ref, o_ref):
    z = z_ref[0]
    tx = z.shape[0]
    lane = lax.broadcasted_iota(jnp.int32, (DN_CHUNK, LANES), 1)
    ri = lax.broadcasted_iota(jnp.int32, (DN_CHUNK, DN_CHUNK), 0)
    ci = lax.broadcasted_iota(jnp.int32, (DN_CHUNK, DN_CHUNK), 1)
    lower = (ri >= ci).astype(F32)
    upper = (ri <= ci).astype(F32)
    beta = jax.nn.sigmoid(z)
    g = nega_ref[...] * _softplus(z + dtb_ref[...])
    for c in range(tx // DN_CHUNK):
        rows = slice(c * DN_CHUNK, (c + 1) * DN_CHUNK)
        gc = g[rows]
        prefix = jnp.dot(lower, gc, precision=HIGHEST, preferred_element_type=F32)
        suffix = jnp.dot(upper, gc, precision=HIGHEST, preferred_element_type=F32)
        tile = jnp.where(lane < 2 * DN_HEADS, beta[rows], jnp.where(lane < 3 * DN_HEADS, prefix, suffix))
        for hp in range(DN_HEADS // 2):
            o_ref[0, hp, rows, :] = jnp.dot(tile, sel_ref[hp], precision=HIGHEST, preferred_element_type=F32)


def _dn_gates(z_ba, neg_a, dt_bias, sel):
    bx, tx, w = z_ba.shape
    return pl.pallas_call(
        _gates_kernel, grid=(bx,),
        in_specs=[pl.BlockSpec((1, tx, w), lambda b: (b, 0, 0)),
                  pl.BlockSpec((1, w), lambda b: (0, 0)),
                  pl.BlockSpec((1, w), lambda b: (0, 0)),
                  pl.BlockSpec(sel.shape, lambda b: (0, 0, 0))],
        out_specs=pl.BlockSpec((1, DN_HEADS // 2, tx, w), lambda b: (b, 0, 0, 0)),
        out_shape=jax.ShapeDtypeStruct((bx, DN_HEADS // 2, tx, w), F32),
        compiler_params=_params(("parallel",), VMEM_LIMIT), name="dn_gates",
    )(z_ba, neg_a, dt_bias, sel)


def _dot_bf16(a, b):
    return jnp.dot(a.astype(BF16), b.astype(BF16), preferred_element_type=F32)


def _chunk_step(q, k, v, k_t, b_col, g_col, state, backward):
    c = DN_CHUNK
    ri = lax.broadcasted_iota(jnp.int32, (c, c), 0)
    ci = lax.broadcasted_iota(jnp.int32, (c, c), 1)
    incl = (ri <= ci) if backward else (ri >= ci)
    strict = (ri < ci) if backward else (ri > ci)
    eye = (ri == ci).astype(F32)
    g_rows = jnp.broadcast_to(g_col, (c, c))
    b_rows = jnp.broadcast_to(b_col, (c, c))
    g_cols = jnp.dot(jnp.ones((c, c), F32), eye * g_rows, precision=HIGHEST,
                     preferred_element_type=F32)
    decay = jnp.where(incl, jnp.exp(jnp.where(incl, g_rows - g_cols, 0.0)), 0.0)
    kb = k * b_rows
    kk = lax.dot_general(jnp.concatenate([kb, q], axis=0).astype(BF16), k.astype(BF16),
                         (((1,), (1,)), ((), ())), preferred_element_type=F32)
    a = jnp.where(strict, kk[:c] * decay, 0.0)
    qk = kk[c:] * decay
    eg = jnp.exp(g_rows)
    x = jnp.concatenate([v * b_rows, kb * eg], axis=1)
    x = x - _dot_bf16(a, x)
    p = a
    for _ in range(5):
        p = _dot_bf16(p, p)
        x = x + _dot_bf16(p, x)
    u = x[:, :DN_HEAD_DIM]
    w = x[:, DN_HEAD_DIM:]
    g_last = jnp.broadcast_to(g_rows[0:1, :] if backward else g_rows[c - 1:c, :], (c, c))
    ws = _dot_bf16(jnp.concatenate([w, q * eg], axis=0), state)
    v_new = u - ws[:c]
    o = ws[c:] + _dot_bf16(qk, v_new)
    k_tail_t = k_t * jnp.exp(g_last - g_cols)
    new_state = state * jnp.exp(g_last) + _dot_bf16(k_tail_t, v_new)
    return o, new_state


def _scan_sequence(q_ref, k_ref, v_ref, g_ref, out_ref, states, backward):
    n_pairs = q_ref.shape[1] // LANES
    d = 1 if backward else 0
    c = DN_CHUNK

    def body(it, carry):
        pair = (n_pairs - 1 - it) if backward else it
        base = pl.multiple_of(pair * LANES, LANES)
        qt = q_ref[0, pl.ds(base, LANES), :]
        kt = k_ref[0, pl.ds(base, LANES), :]
        vt = v_ref[0, pl.ds(base, LANES), :]
        gt = g_ref[0, 0, pl.ds(base, LANES), :]
        kt_t = kt.T
        outs = [[None, None], [None, None]]
        new_states = []
        for hh in range(2):
            st = carry[hh]
            ls = slice(hh * DN_HEAD_DIM, (hh + 1) * DN_HEAD_DIM)
            for half in ((1, 0) if backward else (0, 1)):
                rs = slice(half * c, (half + 1) * c)
                b_col = gt[rs, 2 * d + hh:2 * d + hh + 1]
                g_col = gt[rs, 4 + 2 * d + hh:4 + 2 * d + hh + 1]
                o, st = _chunk_step(qt[rs, ls], kt[rs, ls], vt[rs, ls], kt_t[ls, rs], b_col, g_col, st, backward)
                outs[half][hh] = o
            new_states.append(st)
        if out_ref is not None:
            out_ref[0, pl.ds(base, LANES), :] = jnp.concatenate(
                [jnp.concatenate(outs[0], axis=1), jnp.concatenate(outs[1], axis=1)], axis=0)
        return tuple(new_states)

    return lax.fori_loop(0, n_pairs, body, states)


def _scan_kernel(qc_ref, kc_ref, vc_ref, gc_ref, ql_ref, kl_ref, vl_ref, gl_ref, of_ref, ob_ref):
    zero = jnp.zeros((DN_HEAD_DIM, DN_HEAD_DIM), F32)
    for backward, out_ref in ((False, of_ref), (True, ob_ref)):
        st = _scan_sequence(qc_ref, kc_ref, vc_ref, gc_ref, None, (zero, zero), backward)
        _scan_sequence(ql_ref, kl_ref, vl_ref, gl_ref, out_ref, st, backward)


def _dn_scan(qkv_c, gates_c, qkv_l, gates_l):
    b, t, _ = qkv_l.shape
    tc = qkv_c.shape[1]
    n_hp = DN_HEADS // 2

    def part(tx, off):
        return pl.BlockSpec((1, tx, LANES), lambda bi, hp: (bi, 0, off + hp))

    def gate(tx):
        return pl.BlockSpec((1, 1, tx, LANES), lambda bi, hp: (bi, hp, 0, 0))

    out = pl.BlockSpec((1, t, LANES), lambda bi, hp: (bi, 0, hp))
    return pl.pallas_call(
        _scan_kernel, grid=(b, n_hp),
        in_specs=[part(tc, 0), part(tc, n_hp), part(tc, 2 * n_hp), gate(tc),
                  part(t, 0), part(t, n_hp), part(t, 2 * n_hp), gate(t)],
        out_specs=[out, out],
        out_shape=[jax.ShapeDtypeStruct((b, t, DN_WIDTH), F32)] * 2,
        compiler_params=_params(("parallel", "parallel"), VMEM_LIMIT), name="dn_scan",
    )(qkv_c, qkv_c, qkv_c, gates_c, qkv_l, qkv_l, qkv_l, gates_l)


POST_TM = 256


def _post_kernel(of_ref, ob_ref, dg_ref, ona_ref, gab_ref, x_ref, g1_ref, sh2_ref, sc2_ref,
                 dnw_ref, wa_ref, wb_ref, wo_ref, n2w_ref, wr_ref, br_ref,
                 xl_ref, h2_ref, route_ref, cnt_ref):
    i = pl.program_id(0)

    @pl.when(i == 0)
    def _():
        cnt_ref[...] = jnp.zeros_like(cnt_ref)

    o = of_ref[...] + ob_ref[...]
    hi = lax.broadcasted_iota(jnp.int32, (DN_WIDTH, DN_WIDTH), 0) // DN_HEAD_DIM
    hj = lax.broadcasted_iota(jnp.int32, (DN_WIDTH, DN_WIDTH), 1) // DN_HEAD_DIM
    head_mean = jnp.where(hi == hj, 1.0 / DN_HEAD_DIM, 0.0).astype(F32)
    ms = jnp.dot(o * o, head_mean, precision=HIGHEST, preferred_element_type=F32)
    dg = dg_ref[...]
    o_dn = (o * lax.rsqrt(ms + NORM_EPS) * dnw_ref[...]) * (dg * jax.nn.sigmoid(dg))
    gab = gab_ref[...]
    ya = jnp.dot(ona_ref[...], wa_ref[...], preferred_element_type=F32)
    yb = jnp.dot(o_dn.astype(BF16), wb_ref[...], preferred_element_type=F32)
    y = jax.nn.sigmoid(gab[:, :D_MODEL]) * ya + jax.nn.sigmoid(gab[:, D_MODEL:]) * yb
    y = jnp.dot(y.astype(BF16), wo_ref[...], preferred_element_type=F32)
    xl = x_ref[...] + g1_ref[0] * y
    xl_ref[...] = xl
    ms2 = jnp.mean(xl * xl, axis=-1, keepdims=True)
    h2 = (xl * lax.rsqrt(ms2 + NORM_EPS) * n2w_ref[...]) * (1.0 + sc2_ref[0]) + sh2_ref[0]
    h2_ref[...] = h2

    logits = jnp.dot(h2, wr_ref[...], precision=HIGHEST, preferred_element_type=F32) + br_ref[...]
    tm = logits.shape[0]
    lane = lax.broadcasted_iota(jnp.int32, (tm, LANES), 1).astype(F32)
    vals, idxs = [], []
    cur = logits
    for _ in range(TOP_K):
        m = jnp.max(cur, axis=-1, keepdims=True)
        idx = jnp.min(jnp.where(cur == m, lane, float(LANES)), axis=-1, keepdims=True)
        vals.append(m)
        idxs.append(idx)
        cur = jnp.where(lane == idx, -jnp.inf, cur)
    es = [jnp.exp(v - vals[0]) for v in vals]
    den = es[0] + es[1] + es[2] + es[3]
    onehot = jnp.zeros((tm, LANES), F32)
    for idx in idxs:
        onehot = onehot + jnp.where(lane == idx, 1.0, 0.0)
    ti = lax.broadcasted_iota(jnp.int32, (tm, tm), 0)
    tj = lax.broadcasted_iota(jnp.int32, (tm, tm), 1)
    before = jnp.where(ti > tj, 1.0, 0.0).astype(BF16)
    cnt = cnt_ref[...] + jnp.dot(before, onehot.astype(BF16), preferred_element_type=F32)
    route = jnp.zeros((tm, LANES), F32)
    for kk in range(TOP_K):
        rank = jnp.sum(jnp.where(lane == idxs[kk], cnt, 0.0), axis=-1, keepdims=True)
        route = jnp.where(lane == float(kk), es[kk] / den, route)
        route = jnp.where(lane == float(TOP_K + kk), idxs[kk], route)
        route = jnp.where(lane == float(2 * TOP_K + kk), rank, route)
    route_ref[...] = route
    cnt_ref[...] = cnt_ref[...] + jnp.sum(onehot, axis=0, keepdims=True)


def _post_mixer(o_f, o_b, dg, o_na, gab, x, g1, sh2, sc2, dn_norm_w, w_br_a, w_br_b, w_out, norm2_w, wr, br):
    n, d = x.shape
    tm = POST_TM
    per_batch = SEQ // tm
    tok = lambda w: pl.BlockSpec((tm, w), lambda i: (i, 0))
    row = pl.BlockSpec((1, 1, d), lambda i: (i // per_batch, 0, 0))
    const = lambda a: pl.BlockSpec(a.shape, lambda i: (0,) * a.ndim)
    return pl.pallas_call(
        _post_kernel, grid=(n // tm,),
        in_specs=[tok(DN_WIDTH), tok(DN_WIDTH), tok(DN_WIDTH), tok(NA_WIDTH), tok(2 * D_MODEL), tok(d),
                  row, row, row, const(dn_norm_w), const(w_br_a), const(w_br_b), const(w_out),
                  const(norm2_w), const(wr), const(br)],
        out_specs=[tok(d), tok(d), tok(LANES), pl.BlockSpec((1, LANES), lambda i: (0, 0))],
        out_shape=[jax.ShapeDtypeStruct((n, d), F32), jax.ShapeDtypeStruct((n, d), F32),
                   jax.ShapeDtypeStruct((n, LANES), F32), jax.ShapeDtypeStruct((1, LANES), F32)],
        compiler_params=_params(("arbitrary",), VMEM_LIMIT), name="post_mixer_router",
    )(o_f, o_b, dg, o_na, gab, x, g1, sh2, sc2, dn_norm_w, w_br_a, w_br_b, w_out, norm2_w, wr, br)


def _row_copy(src_hbm, dst_vmem, src_row, dst_row, sem):
    return pltpu.make_async_copy(src_hbm.at[pl.ds(src_row, 1)], dst_vmem.at[pl.ds(dst_row, 1)], sem)


def _expert_kernel(blk_e_ref, nact_ref, tok_ref, h_hbm, w1_ref, b1_ref, w2_ref, b2_ref, y_ref,
                   xbuf, w1b, w2b, sems):
    i = pl.program_id(0)
    n_active = nact_ref[0]
    bm = MOE_BM

    def issue(blk, slot):
        def body(s, carry):
            _row_copy(h_hbm, xbuf.at[slot], tok_ref[blk * bm + s], s, sems.at[slot]).start()
            return carry
        lax.fori_loop(0, bm, body, 0)

    def wait(slot):
        def body(s, carry):
            _row_copy(h_hbm, xbuf.at[slot], 0, s, sems.at[slot]).wait()
            return carry
        lax.fori_loop(0, bm, body, 0)

    @pl.when(i == 0)
    def _():
        issue(0, 0)

    @pl.when(i + 1 < n_active)
    def _():
        issue(i + 1, (i + 1) % 2)

    changed = jnp.logical_or(i == 0, blk_e_ref[i] != blk_e_ref[jnp.maximum(i - 1, 0)])

    @pl.when(jnp.logical_and(i < n_active, changed))
    def _():
        w1b[...] = w1_ref[0].astype(BF16)
        w2b[...] = w2_ref[0].astype(BF16)

    @pl.when(i < n_active)
    def _():
        slot = i % 2
        wait(slot)
        xb = xbuf[slot].astype(BF16)
        hb = jnp.dot(xb, w1b[...], preferred_element_type=F32) + b1_ref[0]
        gate = jnp.minimum(hb[:, :D_EXPERT], SWIGLU_LIMIT)
        up = jnp.clip(hb[:, D_EXPERT:], -SWIGLU_LIMIT, SWIGLU_LIMIT)
        act = (up + 1.0) * gate * jax.nn.sigmoid(SWIGLU_ALPHA * gate)
        y_ref[...] = jnp.dot(act.astype(BF16), w2b[...], preferred_element_type=F32) + b2_ref[0]

    @pl.when(i >= n_active)
    def _():
        y_ref[...] = jnp.zeros_like(y_ref)


def _expert_ffn(blk_expert, n_active, tok_pad, h2, w1, b1, w2, b2):
    n, d = h2.shape
    f2 = w1.shape[2]

    def live(i, nact):
        return jnp.minimum(i, jnp.maximum(nact[0] - 1, 0))

    grid_spec = pltpu.PrefetchScalarGridSpec(
        num_scalar_prefetch=3, grid=(MOE_NBLK,),
        in_specs=[pl.BlockSpec(memory_space=pl.ANY),
                  pl.BlockSpec((1, d, f2), lambda i, be, na, tk: (be[live(i, na)], 0, 0)),
                  pl.BlockSpec((1, 1, f2), lambda i, be, na, tk: (be[live(i, na)], 0, 0)),
                  pl.BlockSpec((1, f2 // 2, d), lambda i, be, na, tk: (be[live(i, na)], 0, 0)),
                  pl.BlockSpec((1, 1, d), lambda i, be, na, tk: (be[live(i, na)], 0, 0))],
        out_specs=pl.BlockSpec((MOE_BM, d), lambda i, be, na, tk: (i, 0)),
        scratch_shapes=[pltpu.VMEM((2, MOE_BM, d), F32), pltpu.VMEM((d, f2), BF16),
                        pltpu.VMEM((f2 // 2, d), BF16), pltpu.SemaphoreType.DMA((2,))])
    return pl.pallas_call(
        _expert_kernel, grid_spec=grid_spec,
        out_shape=jax.ShapeDtypeStruct((MOE_MPAD, d), F32),
        compiler_params=_params(("arbitrary",), VMEM_LIMIT), name="expert_ffn",
    )(blk_expert, n_active, tok_pad, h2, w1, b1, w2, b2)


COMB_TM = 128


def _combine_kernel(dest_ref, y_hbm, xl_ref, route_ref, g2_ref, fw_ref, o_ref, ybuf, sems):
    i = pl.program_id(0)
    n_steps = pl.num_programs(0)
    tm = COMB_TM

    def issue(step, slot):
        def body(t, carry):
            for kk in range(TOP_K):
                _row_copy(y_hbm, ybuf.at[slot], dest_ref[(step * tm + t) * TOP_K + kk], kk * tm + t,
                          sems.at[slot]).start()
            return carry
        lax.fori_loop(0, tm, body, 0)

    def wait(slot):
        def body(t, carry):
            _row_copy(y_hbm, ybuf.at[slot], 0, t, sems.at[slot]).wait()
            return carry
        lax.fori_loop(0, tm * TOP_K, body, 0)

    @pl.when(i == 0)
    def _():
        issue(0, 0)

    @pl.when(i + 1 < n_steps)
    def _():
        issue(i + 1, (i + 1) % 2)

    slot = i % 2
    wait(slot)
    route = route_ref[...]
    moe = jnp.zeros((tm, D_MODEL), F32)
    for kk in range(TOP_K):
        moe = moe + route[:, kk:kk + 1] * ybuf[slot, pl.ds(kk * tm, tm), :]
    xo = xl_ref[...] + g2_ref[0] * moe
    ms = jnp.mean(xo * xo, axis=-1, keepdims=True)
    o_ref[...] = xo * lax.rsqrt(ms + NORM_EPS) * fw_ref[...]


def _combine(dest, y_sorted, xl, route, g2, final_w):
    n, d = xl.shape
    tm = COMB_TM
    per_batch = SEQ // tm
    grid_spec = pltpu.PrefetchScalarGridSpec(
        num_scalar_prefetch=1, grid=(n // tm,),
        in_specs=[pl.BlockSpec(memory_space=pl.ANY),
                  pl.BlockSpec((tm, d), lambda i, ds: (i, 0)),
                  pl.BlockSpec((tm, LANES), lambda i, ds: (i, 0)),
                  pl.BlockSpec((1, 1, d), lambda i, ds: (i // per_batch, 0, 0)),
                  pl.BlockSpec((1, d), lambda i, ds: (0, 0))],
        out_specs=pl.BlockSpec((tm, d), lambda i, ds: (i, 0)),
        scratch_shapes=[pltpu.VMEM((2, TOP_K * tm, d), F32), pltpu.SemaphoreType.DMA((2,))])
    return pl.pallas_call(
        _combine_kernel, grid_spec=grid_spec,
        out_shape=jax.ShapeDtypeStruct((n, d), F32),
        compiler_params=_params(("arbitrary",), VMEM_LIMIT), name="moe_combine",
    )(dest, y_sorted, xl, route, g2, final_w)


def _split_in_weights(w_in):
    o = np.cumsum((0, NA_WIDTH, NA_WIDTH, NA_WIDTH, 3 * DN_WIDTH, DN_WIDTH, 2 * DN_HEADS, 2 * DN_HEADS,
                   D_MODEL, D_MODEL))
    wb = w_in.astype(BF16)
    w_na = wb[:, o[0]:o[3]]
    w_dq = wb[:, o[3]:o[4]]
    w_dg = wb[:, o[4]:o[5]]
    w_gab = wb[:, o[7]:o[9]]
    w_dba = jnp.pad(wb[:, o[5]:o[7]], ((0, 0), (0, LANES - 4 * DN_HEADS)))
    return w_na, w_dq, w_dg, w_gab, w_dba


def kernel(x, c, ctx, c_ctx, w_mod, b_mod, norm1_w, w_in, na_rpb, dn_conv_w, dn_a_log, dn_dt_bias, dn_norm_w,
           w_br_a, w_br_b, w_out, norm2_w, w_router, b_router, w1, b1, w2, b2, final_norm_w):
    d = D_MODEL
    cc = jnp.concatenate([c, c_ctx[None], jnp.zeros((16 - BATCH - 1, d), F32)], axis=0)
    mod = _modulation(cc, w_mod[0], b_mod[0][None])
    mod_l = mod[:BATCH].reshape(BATCH, 6, 1, d)
    sh1, sc1, g1, sh2, sc2, g2 = (mod_l[:, i] for i in range(6))
    mod_c = jnp.broadcast_to(mod[BATCH].reshape(6, 1, 1, d), (6, BATCH, 1, d))

    weights = _split_in_weights(w_in[0])
    n1w = norm1_w[0][None]
    naq, nak, nav, dqkv_l, dg_l, gab_l, dba_l = _in_projection(x, sh1, sc1, n1w, weights, 512)
    _, nak_c, nav_c, dqkv_c, _, _, dba_c = _in_projection(ctx, mod_c[0], mod_c[1], n1w, weights, CTX_LEN)

    o_na = _neighborhood_attention(naq, nak, nav, nak_c, nav_c, _na_bias_table(na_rpb[0]))

    cos, sin = _rope_tables(SEQ)
    qkv_l = _dn_prepare(dqkv_l, dn_conv_w[0], cos, sin, True)
    qkv_c = _dn_prepare(dqkv_c, dn_conv_w[0], cos[:CTX_LEN], sin[:CTX_LEN], False)
    pad16 = lambda v: jnp.pad(v.reshape(1, 2 * DN_HEADS), ((0, 0), (2 * DN_HEADS, LANES - 4 * DN_HEADS)))
    neg_a = pad16(-jnp.exp(dn_a_log[0]))
    dtb = pad16(dn_dt_bias[0])
    sel = _gate_select_matrices()
    gates_l = _dn_gates(dba_l, neg_a, dtb, sel)
    gates_c = _dn_gates(dba_c, neg_a, dtb, sel)
    o_f, o_b = _dn_scan(qkv_c, gates_c, qkv_l, gates_l)

    flat = lambda a: a.reshape(N_TOK, a.shape[-1])
    wr = jnp.pad(w_router[0], ((0, 0), (0, LANES - N_EXPERTS)))
    br = jnp.concatenate([b_router[0], jnp.full((LANES - N_EXPERTS,), NEG_INF, F32)])[None]
    dnw = jnp.tile(dn_norm_w[0], DN_HEADS)[None]
    xl, h2, route, counts = _post_mixer(
        flat(o_f), flat(o_b), flat(dg_l), flat(o_na), flat(gab_l), flat(x), g1, sh2, sc2, dnw,
        w_br_a[0].astype(BF16), w_br_b[0].astype(BF16), w_out[0].astype(BF16), norm2_w[0][None], wr, br)

    e_idx = route[:, TOP_K:2 * TOP_K].astype(jnp.int32)
    rank = route[:, 2 * TOP_K:3 * TOP_K].astype(jnp.int32)
    cnt = counts[0, :N_EXPERTS].astype(jnp.int32)
    padded = (cnt + MOE_BM - 1) // MOE_BM * MOE_BM
    pad_ends = jnp.cumsum(padded)
    pad_starts = pad_ends - padded
    dest = (pad_starts[e_idx] + rank).reshape(-1)
    tok_pad = jnp.zeros((MOE_MPAD,), jnp.int32).at[dest].set(jnp.arange(N_ASG, dtype=jnp.int32) // TOP_K)
    blk_first_row = jnp.arange(MOE_NBLK, dtype=jnp.int32) * MOE_BM
    blk_expert = jnp.minimum(jnp.sum((pad_ends[None, :] <= blk_first_row[:, None]).astype(jnp.int32), axis=1),
                             N_EXPERTS - 1)
    n_active = (pad_ends[-1:] // MOE_BM).astype(jnp.int32)

    y_sorted = _expert_ffn(blk_expert, n_active, tok_pad, h2, w1[0], b1[0][:, None], w2[0], b2[0][:, None])
    out = _combine(dest, y_sorted, xl, route, g2, final_norm_w[None])
    return out.reshape(BATCH, SEQ, d)
```

```python
import functools

import numpy as np
import jax
import jax.numpy as jnp
from jax import lax
from jax.experimental import pallas as pl
from jax.experimental.pallas import tpu as pltpu

F32 = jnp.float32
BF16 = jnp.bfloat16
HIGHEST = lax.Precision.HIGHEST

D_MODEL = 1024
BATCH = 8
SEQ = 2048
GRID_W = 64
GRID_ROWS = SEQ // GRID_W
CTX_LEN = 256
NA_HEADS = 8
NA_HEAD_DIM = 64
NA_KH = 8
NA_KW = 16
NA_SCALE = NA_HEAD_DIM ** -0.5
NA_WIDTH = NA_HEADS * NA_HEAD_DIM
DN_HEADS = 8
DN_HEAD_DIM = 64
DN_WIDTH = DN_HEADS * DN_HEAD_DIM
DN_CONV = 5
DN_CHUNK = 64
ROPE_THETA = 10000.0
N_EXPERTS = 32
TOP_K = 4
D_EXPERT = 1024
SWIGLU_LIMIT = 7.0
SWIGLU_ALPHA = 1.702
NORM_EPS = 1e-6
NEG_INF = -1e30

LANES = 128
N_TOK = BATCH * SEQ
N_ASG = N_TOK * TOP_K
MOE_BM = 256
MOE_MPAD = -(-(N_ASG + N_EXPERTS * (MOE_BM - 1)) // MOE_BM) * MOE_BM
MOE_NBLK = MOE_MPAD // MOE_BM
VMEM_LIMIT = 56 * 1024 * 1024

DN_T = CTX_LEN + SEQ
DN_NCHUNK = DN_T // DN_CHUNK
DN_CTX_CHUNKS = CTX_LEN // DN_CHUNK
DN_PAIRS = DN_HEADS // 2


def _params(sem, vmem=None):
    return pltpu.CompilerParams(dimension_semantics=sem, vmem_limit_bytes=vmem)


def _mod_kernel(c_ref, w_ref, b_ref, o_ref):
    c = c_ref[...]
    s = c * jax.nn.sigmoid(c)
    o_ref[...] = jnp.dot(s, w_ref[...], precision=HIGHEST, preferred_element_type=F32) + b_ref[...]


def _modulation(cc, w_mod, b_mod):
    rows, d = cc.shape
    n = w_mod.shape[1]
    bn = 1024
    return pl.pallas_call(
        _mod_kernel, grid=(n // bn,),
        in_specs=[pl.BlockSpec((rows, d), lambda j: (0, 0)),
                  pl.BlockSpec((d, bn), lambda j: (0, j)),
                  pl.BlockSpec((1, bn), lambda j: (0, j))],
        out_specs=pl.BlockSpec((rows, bn), lambda j: (0, j)),
        out_shape=jax.ShapeDtypeStruct((rows, n), F32),
        compiler_params=_params(("parallel",)), name="modulation")(cc, w_mod, b_mod)


def _inproj_kernel(x_ref, sh_ref, sc_ref, nw_ref, wna_ref, wdq_ref, wdg_ref, wgab_ref, wdba_ref,
                   q_ref, k_ref, v_ref, dq_ref, dg_ref, gab_ref, dba_ref):
    x = x_ref[0]
    ms = jnp.mean(x * x, axis=-1, keepdims=True)
    y = x * lax.rsqrt(ms + NORM_EPS) * nw_ref[...]
    h = (y * (1.0 + sc_ref[0]) + sh_ref[0]).astype(BF16)
    na = jnp.dot(h, wna_ref[...], preferred_element_type=F32)
    q_ref[0] = (na[:, :NA_WIDTH] * NA_SCALE).astype(BF16)
    k_ref[0] = na[:, NA_WIDTH:2 * NA_WIDTH].astype(BF16)
    v_ref[0] = na[:, 2 * NA_WIDTH:].astype(BF16)
    dq_ref[0] = jnp.dot(h, wdq_ref[...], preferred_element_type=F32)
    dg_ref[0] = jnp.dot(h, wdg_ref[...], preferred_element_type=F32)
    gab_ref[0] = jnp.dot(h, wgab_ref[...], preferred_element_type=F32)
    dba_ref[0] = jnp.dot(h, wdba_ref[...], preferred_element_type=F32)


def _in_projection(x, shift, scale, norm_w, weights, tm):
    bx, tx, d = x.shape
    wna, wdq, wdg, wgab, wdba = weights
    tok = lambda w: pl.BlockSpec((1, tm, w), lambda b, i: (b, i, 0))
    row = pl.BlockSpec((1, 1, d), lambda b, i: (b, 0, 0))
    const = lambda a: pl.BlockSpec(a.shape, lambda b, i: (0,) * a.ndim)
    widths = (NA_WIDTH, NA_WIDTH, NA_WIDTH, 3 * DN_WIDTH, DN_WIDTH, 2 * D_MODEL, LANES)
    dtypes = (BF16, BF16, BF16, F32, F32, F32, F32)
    return pl.pallas_call(
        _inproj_kernel, grid=(bx, tx // tm),
        in_specs=[tok(d), row, row, const(norm_w), const(wna), const(wdq), const(wdg), const(wgab), const(wdba)],
        out_specs=[tok(w) for w in widths],
        out_shape=[jax.ShapeDtypeStruct((bx, tx, w), dt) for w, dt in zip(widths, dtypes)],
        compiler_params=_params(("parallel", "parallel"), VMEM_LIMIT), name="in_projection",
    )(x, shift, scale, norm_w, wna, wdq, wdg, wgab, wdba)


def _na_bias_table(rpb):
    col = np.arange(GRID_W)
    col_start = np.clip(col - NA_KW // 2, 0, GRID_W - NA_KW)
    mask = (col[None, :] >= col_start[:, None]) & (col[None, :] < col_start[:, None] + NA_KW)
    dc = np.clip(col[None, :] - col[:, None] + NA_KW - 1, 0, 2 * NA_KW - 2)
    pick = np.zeros((2 * NA_KW - 1, GRID_W * GRID_W), np.float32)
    pick[dc.reshape(-1), np.arange(GRID_W * GRID_W)] = 1.0
    n_dr = 2 * NA_KH - 1
    by_dr = jnp.dot(rpb.astype(F32).reshape(NA_HEADS * n_dr, 2 * NA_KW - 1), pick, precision=HIGHEST)
    by_dr = jnp.where(mask[None, None], by_dr.reshape(NA_HEADS, n_dr, GRID_W, GRID_W), NEG_INF)
    tbl = jnp.stack([by_dr[:, c:c + NA_KH] for c in range(NA_KH)], axis=0)
    return tbl.transpose(0, 1, 3, 2, 4).reshape(NA_KH, NA_HEADS, GRID_W, NA_KH * GRID_W)


def _na_first_row(r):
    return jnp.clip(r - NA_KH // 2, 0, GRID_ROWS - NA_KH)


def _na_kernel(q_ref, k_ref, v_ref, kc_ref, vc_ref, bias_ref, o_ref):
    r = pl.program_id(1)
    start = pl.multiple_of(_na_first_row(r) * GRID_W, GRID_W)
    q = q_ref[0]
    kw = k_ref[0, pl.ds(start, NA_KH * GRID_W), :]
    vw = v_ref[0, pl.ds(start, NA_KH * GRID_W), :]
    kc = kc_ref[0]
    vc = vc_ref[0]
    nt = (((1,), (1,)), ((), ()))
    outs = []
    for h in range(NA_HEADS):
        sl = slice(h * NA_HEAD_DIM, (h + 1) * NA_HEAD_DIM)
        qh = q[:, sl]
        s = lax.dot_general(qh, kw[:, sl], nt, preferred_element_type=F32) + bias_ref[0, h]
        sc = lax.dot_general(qh, kc[:, sl], nt, preferred_element_type=F32)
        m = jnp.maximum(jnp.max(s, axis=-1, keepdims=True), jnp.max(sc, axis=-1, keepdims=True))
        p = jnp.exp(s - m)
        pc = jnp.exp(sc - m)
        denom = jnp.sum(p, axis=-1, keepdims=True) + jnp.sum(pc, axis=-1, keepdims=True)
        o = (jnp.dot(p.astype(BF16), vw[:, sl], preferred_element_type=F32)
             + jnp.dot(pc.astype(BF16), vc[:, sl], preferred_element_type=F32))
        outs.append(o / denom)
    o_ref[0] = jnp.concatenate(outs, axis=1).astype(BF16)


def _neighborhood_attention(q, k, v, kc, vc, bias):
    b, t, w = q.shape

    def bias_map(bi, r):
        return (_na_first_row(r) - r + NA_KH - 1, 0, 0, 0)

    return pl.pallas_call(
        _na_kernel, grid=(b, GRID_ROWS),
        in_specs=[pl.BlockSpec((1, GRID_W, w), lambda bi, r: (bi, r, 0)),
                  pl.BlockSpec((1, t, w), lambda bi, r: (bi, 0, 0)),
                  pl.BlockSpec((1, t, w), lambda bi, r: (bi, 0, 0)),
                  pl.BlockSpec((1, CTX_LEN, w), lambda bi, r: (bi, 0, 0)),
                  pl.BlockSpec((1, CTX_LEN, w), lambda bi, r: (bi, 0, 0)),
                  pl.BlockSpec((1, NA_HEADS, GRID_W, NA_KH * GRID_W), bias_map)],
        out_specs=pl.BlockSpec((1, GRID_W, w), lambda bi, r: (bi, r, 0)),
        out_shape=jax.ShapeDtypeStruct((b, t, w), BF16),
        compiler_params=_params(("parallel", "parallel"), VMEM_LIMIT), name="neighborhood_attention",
    )(q, k, v, kc, vc, bias)


def _rope_tables(t_len):
    t = jnp.arange(t_len)
    row = (t // GRID_W).astype(F32)
    col = (t % GRID_W).astype(F32)
    n_axis = DN_HEAD_DIM // 4
    freqs = ROPE_THETA ** (-jnp.arange(n_axis, dtype=F32) / n_axis)
    ang = jnp.concatenate([row[:, None] * freqs, col[:, None] * freqs], axis=-1)
    cos = jnp.tile(jnp.cos(ang), (1, 4))
    sin = jnp.sin(ang)
    sin_signed = jnp.tile(jnp.concatenate([-sin, sin], axis=-1), (1, 2))
    return cos, sin_signed


def _conv_silu(x, w):
    tx = x.shape[0]
    t = lax.broadcasted_iota(jnp.int32, x.shape, 0)
    acc = x * w[DN_CONV // 2:DN_CONV // 2 + 1, :]
    for tap in range(DN_CONV):
        shift = DN_CONV // 2 - tap
        if shift == 0:
            continue
        xs = pltpu.roll(x, shift % tx, axis=0)
        valid = (t >= shift) & (t < tx + shift)
        acc = acc + jnp.where(valid, xs, 0.0) * w[tap:tap + 1, :]
    return acc * jax.nn.sigmoid(acc)


def _dnprep_kernel(cq_ref, ck_ref, cv_ref, lq_ref, lk_ref, lv_ref, wq_ref, wk_ref, wv_ref, cos_ref, sin_ref,
                   vk_ref, qp_ref, kt_ref):
    li = lax.broadcasted_iota(jnp.int32, (LANES, LANES), 0) // DN_HEAD_DIM
    lj = lax.broadcasted_iota(jnp.int32, (LANES, LANES), 1) // DN_HEAD_DIM
    same_head = (li == lj).astype(F32)
    half = DN_HEAD_DIM // 2

    def l2norm(y):
        ss = jnp.dot(y * y, same_head, precision=HIGHEST, preferred_element_type=F32)
        return y * lax.rsqrt(ss + NORM_EPS)

    def rope(y):
        lane = lax.broadcasted_iota(jnp.int32, y.shape, 1)
        partner = jnp.where(lane % DN_HEAD_DIM < half,
                            pltpu.roll(y, LANES - half, axis=1), pltpu.roll(y, half, axis=1))
        return y * cos_ref[...] + partner * sin_ref[...]

    off = 0
    for q_ref, k_ref, v_ref, use_rope in ((cq_ref, ck_ref, cv_ref, False), (lq_ref, lk_ref, lv_ref, True)):
        q = l2norm(_conv_silu(q_ref[0], wq_ref[...]))
        k = l2norm(_conv_silu(k_ref[0], wk_ref[...]))
        v = _conv_silu(v_ref[0], wv_ref[...])
        if use_rope:
            q = rope(q)
            k = rope(k)
        q = q * DN_HEAD_DIM ** -0.5
        tx = q.shape[0]
        rows = slice(off, off + tx)
        left = lax.broadcasted_iota(jnp.int32, q.shape, 1) < DN_HEAD_DIM
        k_sw = pltpu.roll(k, DN_HEAD_DIM, axis=1)
        q_sw = pltpu.roll(q, DN_HEAD_DIM, axis=1)
        vk_ref[0, 0, rows, :] = jnp.where(left, v, k_sw)
        vk_ref[0, 1, rows, :] = jnp.where(left, k_sw, v)
        qp_ref[0, 0, rows, :] = jnp.where(left, 0.0, q_sw)
        qp_ref[0, 1, rows, :] = jnp.where(left, q_sw, 0.0)
        kt_ref[0, 0, :, rows] = k_sw.T
        off += tx


def _dn_prepare(z_c, z_l, conv_w, cos, sin):
    b = z_l.shape[0]
    n_hp = DN_PAIRS

    def part(tx, off):
        return pl.BlockSpec((1, tx, LANES), lambda bi, hp: (bi, 0, off + hp))

    def wpart(off):
        return pl.BlockSpec((DN_CONV, LANES), lambda bi, hp: (0, off + hp))

    tbl = pl.BlockSpec((SEQ, LANES), lambda bi, hp: (0, 0))
    per_head = pl.BlockSpec((1, 2, DN_T, LANES), lambda bi, hp: (bi, hp, 0, 0))
    return pl.pallas_call(
        _dnprep_kernel, grid=(b, n_hp),
        in_specs=[part(CTX_LEN, 0), part(CTX_LEN, n_hp), part(CTX_LEN, 2 * n_hp),
                  part(SEQ, 0), part(SEQ, n_hp), part(SEQ, 2 * n_hp),
                  wpart(0), wpart(n_hp), wpart(2 * n_hp), tbl, tbl],
        out_specs=[per_head, per_head, pl.BlockSpec((1, 1, LANES, DN_T), lambda bi, hp: (bi, hp, 0, 0))],
        out_shape=[jax.ShapeDtypeStruct((b, DN_HEADS, DN_T, LANES), F32),
                   jax.ShapeDtypeStruct((b, DN_HEADS, DN_T, LANES), F32),
                   jax.ShapeDtypeStruct((b, n_hp, LANES, DN_T), F32)],
        compiler_params=_params(("parallel", "parallel"), VMEM_LIMIT), name="dn_prepare",
    )(z_c, z_c, z_c, z_l, z_l, z_l, conv_w, conv_w, conv_w, cos, sin)


def _gate_select_matrices():
    sel = np.zeros((DN_PAIRS, LANES, LANES), np.float32)
    for hp in range(DN_PAIRS):
        for d in range(2):
            for hh in range(2):
                src = d * DN_HEADS + 2 * hp + hh
                sel[hp, src, 2 * d + hh] = 1.0
                sel[hp, 2 * DN_HEADS + src, 4 + 2 * d + hh] = 1.0
    return jnp.asarray(sel)


def _softplus(x):
    return jnp.maximum(x, 0.0) + jnp.log1p(jnp.exp(-jnp.abs(x)))


def _gates_kernel(zc_ref, zl_ref, nega_ref, dtb_ref, sel_ref, gc_ref, gr_ref):
    z = jnp.concatenate([zc_ref[0], zl_ref[0]], axis=0)
    lane = lax.broadcasted_iota(jnp.int32, (DN_CHUNK, LANES), 1)
    ri = lax.broadcasted_iota(jnp.int32, (DN_CHUNK, DN_CHUNK), 0)
    ci = lax.broadcasted_iota(jnp.int32, (DN_CHUNK, DN_CHUNK), 1)
    lower = (ri >= ci).astype(F32)
    upper = (ri <= ci).astype(F32)
    beta = jax.nn.sigmoid(z)
    g = nega_ref[...] * _softplus(z + dtb_ref[...])
    tiles = []
    for c in range(DN_NCHUNK):
        rows = slice(c * DN_CHUNK, (c + 1) * DN_CHUNK)
        gc = g[rows]
        prefix = jnp.dot(lower, gc, precision=HIGHEST, preferred_element_type=F32)
        suffix = jnp.dot(upper, gc, precision=HIGHEST, preferred_element_type=F32)
        tiles.append(jnp.where(lane < 2 * DN_HEADS, beta[rows], jnp.where(lane < 3 * DN_HEADS, prefix, suffix)))
    tile = jnp.concatenate(tiles, axis=0)
    for hp in range(DN_PAIRS):
        cols = jnp.dot(tile, sel_ref[hp], precision=HIGHEST, preferred_element_type=F32)
        gc_ref[0, hp] = cols
        gr_ref[0, hp] = cols.T[0:8, :]


def _dn_gates(z_c, z_l, neg_a, dt_bias, sel):
    b = z_l.shape[0]
    return pl.pallas_call(
        _gates_kernel, grid=(b,),
        in_specs=[pl.BlockSpec((1, CTX_LEN, LANES), lambda bi: (bi, 0, 0)),
                  pl.BlockSpec((1, SEQ, LANES), lambda bi: (bi, 0, 0)),
                  pl.BlockSpec((1, LANES), lambda bi: (0, 0)),
                  pl.BlockSpec((1, LANES), lambda bi: (0, 0)),
                  pl.BlockSpec(sel.shape, lambda bi: (0, 0, 0))],
        out_specs=[pl.BlockSpec((1, DN_PAIRS, DN_T, LANES), lambda bi: (bi, 0, 0, 0)),
                   pl.BlockSpec((1, DN_PAIRS, 8, DN_T), lambda bi: (bi, 0, 0, 0))],
        out_shape=[jax.ShapeDtypeStruct((b, DN_PAIRS, DN_T, LANES), F32),
                   jax.ShapeDtypeStruct((b, DN_PAIRS, 8, DN_T), F32)],
        compiler_params=_params(("parallel",), VMEM_LIMIT), name="dn_gates",
    )(z_c, z_l, neg_a, dt_bias, sel)


def _dot_bf16(a, b):
    return jnp.dot(a.astype(BF16), b.astype(BF16), preferred_element_type=F32)


def _block_masks():
    ri = lax.broadcasted_iota(jnp.int32, (LANES, LANES), 0)
    ci = lax.broadcasted_iota(jnp.int32, (LANES, LANES), 1)
    same = (ri // DN_CHUNK) == (ci // DN_CHUNK)
    rin = ri % DN_CHUNK
    cin = ci % DN_CHUNK
    incl = (same & (rin >= cin), same & (rin <= cin))
    strict = (same & (rin > cin), same & (rin < cin))
    return same, incl, strict


def _scan_phase_a(p, vk_ref, qp_ref, kt_ref, gc_ref, gr_ref, xs_ref, lws_ref, lfin_ref, gl_ref):
    c = DN_CHUNK
    same, incl, strict = _block_masks()
    other = jnp.logical_not(same)
    rows = pl.ds(pl.multiple_of(p * LANES, LANES), LANES)
    vk_a = vk_ref[0, 0, rows, :]
    vk_b = vk_ref[0, 1, rows, :]
    qp_a = qp_ref[0, 0, rows, :]
    qp_b = qp_ref[0, 1, rows, :]
    kts = kt_ref[0, 0, :, rows]
    ktr = pltpu.roll(kts, c, axis=1)
    gct = gc_ref[0, 0, rows, :]
    grt = gr_ref[0, 0, :, rows]
    grr = pltpu.roll(grt, c, axis=1)
    lane8 = lax.broadcasted_iota(jnp.int32, (8, LANES), 1)
    left8 = lane8 < c

    items = []
    for par in range(2):
        hs = slice(par * c, (par + 1) * c)
        vks = jnp.concatenate([vk_a[hs], vk_b[hs]], axis=0)
        qq = jnp.concatenate([qp_a[hs], qp_b[hs]], axis=0)
        kk_in = jnp.where(other, vks, 0.0)
        kk = lax.dot_general(jnp.concatenate([kk_in, qq], axis=0).astype(BF16), kk_in.astype(BF16),
                             (((1,), (1,)), ((), ())), preferred_element_type=F32)
        kt_raw = (jnp.concatenate([ktr[:c], kts[c:]], axis=0) if par == 0
                  else jnp.concatenate([kts[:c], ktr[c:]], axis=0))
        for d in range(2):
            bcol = lambda col: jnp.broadcast_to(gct[hs, col:col + 1], (c, LANES))
            beta = jnp.concatenate([bcol(2 * d), bcol(2 * d + 1)], axis=0)
            g_rows = jnp.concatenate([bcol(4 + 2 * d), bcol(5 + 2 * d)], axis=0)
            if par == 0:
                g_lane = jnp.where(left8, grt[4 + 2 * d:5 + 2 * d], grr[5 + 2 * d:6 + 2 * d])
            else:
                g_lane = jnp.where(left8, grr[4 + 2 * d:5 + 2 * d], grt[5 + 2 * d:6 + 2 * d])
            last = par * c + (c - 1 if d == 0 else 0)
            gl_a = jnp.broadcast_to(grt[4 + 2 * d:5 + 2 * d, last:last + 1], (8, LANES))
            gl_b = jnp.broadcast_to(grt[5 + 2 * d:6 + 2 * d, last:last + 1], (8, LANES))
            g_last = jnp.where(left8, gl_a, gl_b)
            g_cols = jnp.broadcast_to(g_lane[0:1], (LANES, LANES))
            decay = jnp.where(incl[d], jnp.exp(jnp.where(incl[d], g_rows - g_cols, 0.0)), 0.0)
            a = jnp.where(strict[d], kk[:LANES] * decay, 0.0) * beta
            qk = kk[LANES:] * decay
            eg = jnp.exp(g_rows)
            x = vks * beta * jnp.where(other, eg, 1.0)
            qe = qq * eg
            kt_s = jnp.where(other, kt_raw * jnp.exp(jnp.broadcast_to(g_last[0:1] - g_lane[0:1], (LANES, LANES))),
                             0.0)
            items.append(dict(idx=(2 * p + par) * 2 + d, p=a, x=x, qe=qe, qk=qk, kt=kt_s, gl=jnp.exp(g_last)))

    for step in range(6):
        for it in items:
            pb = it["p"].astype(BF16)
            if step < 5:
                r = jnp.dot(pb, jnp.concatenate([it["p"], it["x"]], axis=1).astype(BF16),
                            preferred_element_type=F32)
                it["p"] = r[:, :LANES]
                px = r[:, LANES:]
            else:
                px = jnp.dot(pb, it["x"].astype(BF16), preferred_element_type=F32)
            it["x"] = it["x"] - px if step == 0 else it["x"] + px
    for it in items:
        idx = it["idx"]
        xs_ref[idx] = it["x"]
        lws_ref[idx] = jnp.concatenate([it["x"], it["qe"]], axis=0).astype(BF16)
        lfin_ref[idx] = jnp.concatenate([it["qk"], it["kt"]], axis=0).astype(BF16)
        gl_ref[idx] = it["gl"]


def _scan_phase_b(lo, hi, states, xs_ref, lws_ref, lfin_ref, gl_ref, out_refs):
    c = DN_CHUNK
    same, _, _ = _block_masks()
    other = jnp.logical_not(same)
    left = lax.broadcasted_iota(jnp.int32, (c, LANES), 1) < c

    def body(s, carry):
        u_f = s
        u_b = jnp.where(s < DN_CTX_CHUNKS, DN_CTX_CHUNKS - 1 - s, DN_NCHUNK + DN_CTX_CHUNKS - 1 - s)
        us = (u_f, u_b)
        idx = [us[d] * 2 + d for d in range(2)]
        res = [jnp.dot(lws_ref[idx[d]], carry[d].astype(BF16), preferred_element_type=F32) for d in range(2)]
        vn = [(xs_ref[idx[d]] - res[d][:LANES]).astype(BF16) for d in range(2)]
        fin = [jnp.dot(lfin_ref[idx[d]], vn[d], preferred_element_type=F32) for d in range(2)]
        new = []
        for d in range(2):
            if out_refs is not None:
                o = jnp.where(left, fin[d][:c] + res[d][LANES:LANES + c],
                              fin[d][c:LANES] + res[d][LANES + c:])
                out_refs[d][0, pl.ds(pl.multiple_of((us[d] - DN_CTX_CHUNKS) * c, c), c), :] = o
            decay = jnp.broadcast_to(gl_ref[idx[d]][0:1], (LANES, LANES))
            new.append(carry[d] * decay + jnp.where(other, fin[d][LANES:], 0.0))
        return tuple(new)

    return lax.fori_loop(lo, hi, body, states)


def _scan_kernel(vk_ref, qp_ref, kt_ref, gc_ref, gr_ref, of_ref, ob_ref, xs_ref, lws_ref, lfin_ref, gl_ref):
    def phase_a(p, carry):
        _scan_phase_a(p, vk_ref, qp_ref, kt_ref, gc_ref, gr_ref, xs_ref, lws_ref, lfin_ref, gl_ref)
        return carry

    lax.fori_loop(0, DN_T // LANES, phase_a, 0)
    zero = jnp.zeros((LANES, LANES), F32)
    scratch = (xs_ref, lws_ref, lfin_ref, gl_ref)
    st = _scan_phase_b(0, DN_CTX_CHUNKS, (zero, zero), *scratch, None)
    _scan_phase_b(DN_CTX_CHUNKS, DN_NCHUNK, st, *scratch, (of_ref, ob_ref))


def _dn_scan(vk, qp, kt, g_cols, g_rows):
    b = vk.shape[0]
    n_items = 2 * DN_NCHUNK
    out = pl.BlockSpec((1, SEQ, LANES), lambda bi, hp: (bi, 0, hp))
    return pl.pallas_call(
        _scan_kernel, grid=(b, DN_PAIRS),
        in_specs=[pl.BlockSpec((1, 2, DN_T, LANES), lambda bi, hp: (bi, hp, 0, 0)),
                  pl.BlockSpec((1, 2, DN_T, LANES), lambda bi, hp: (bi, hp, 0, 0)),
                  pl.BlockSpec((1, 1, LANES, DN_T), lambda bi, hp: (bi, hp, 0, 0)),
                  pl.BlockSpec((1, 1, DN_T, LANES), lambda bi, hp: (bi, hp, 0, 0)),
                  pl.BlockSpec((1, 1, 8, DN_T), lambda bi, hp: (bi, hp, 0, 0))],
        out_specs=[out, out],
        out_shape=[jax.ShapeDtypeStruct((b, SEQ, DN_WIDTH), F32)] * 2,
        scratch_shapes=[pltpu.VMEM((n_items, LANES, LANES), F32),
                        pltpu.VMEM((n_items, 2 * LANES, LANES), BF16),
                        pltpu.VMEM((n_items, 2 * LANES, LANES), BF16),
                        pltpu.VMEM((n_items, 8, LANES), F32)],
        compiler_params=_params(("parallel", "parallel"), VMEM_LIMIT), name="dn_scan",
    )(vk, qp, kt, g_cols, g_rows)


POST_TM = 256
ROW_TILE = 8


def _store_row_tiles(ref, val):
    m = val.shape[0]
    for j in range(ROW_TILE):
        ref[pl.ds(j, m, stride=ROW_TILE), :] = val[:, j * LANES:(j + 1) * LANES]


def _load_row_tiles(ref, first, m):
    return jnp.concatenate([ref[pl.ds(first * ROW_TILE + j, m, stride=ROW_TILE), :] for j in range(ROW_TILE)],
                           axis=1)


def _post_kernel(of_ref, ob_ref, dg_ref, ona_ref, gab_ref, x_ref, g1_ref, sh2_ref, sc2_ref,
                 dnw_ref, wa_ref, wb_ref, wo_ref, n2w_ref, wr_ref, br_ref,
                 xl_ref, h2_ref, route_ref, cnt_ref):
    i = pl.program_id(0)

    @pl.when(i == 0)
    def _():
        cnt_ref[...] = jnp.zeros_like(cnt_ref)

    o = of_ref[...] + ob_ref[...]
    hi = lax.broadcasted_iota(jnp.int32, (DN_WIDTH, DN_WIDTH), 0) // DN_HEAD_DIM
    hj = lax.broadcasted_iota(jnp.int32, (DN_WIDTH, DN_WIDTH), 1) // DN_HEAD_DIM
    head_mean = jnp.where(hi == hj, 1.0 / DN_HEAD_DIM, 0.0).astype(F32)
    ms = jnp.dot(o * o, head_mean, precision=HIGHEST, preferred_element_type=F32)
    dg = dg_ref[...]
    o_dn = (o * lax.rsqrt(ms + NORM_EPS) * dnw_ref[...]) * (dg * jax.nn.sigmoid(dg))
    gab = gab_ref[...]
    ya = jnp.dot(ona_ref[...], wa_ref[...], preferred_element_type=F32)
    yb = jnp.dot(o_dn.astype(BF16), wb_ref[...], preferred_element_type=F32)
    y = jax.nn.sigmoid(gab[:, :D_MODEL]) * ya + jax.nn.sigmoid(gab[:, D_MODEL:]) * yb
    y = jnp.dot(y.astype(BF16), wo_ref[...], preferred_element_type=F32)
    xl = x_ref[...] + g1_ref[0] * y
    xl_ref[...] = xl
    ms2 = jnp.mean(xl * xl, axis=-1, keepdims=True)
    h2 = (xl * lax.rsqrt(ms2 + NORM_EPS) * n2w_ref[...]) * (1.0 + sc2_ref[0]) + sh2_ref[0]
    _store_row_tiles(h2_ref, h2)

    logits = jnp.dot(h2, wr_ref[...], precision=HIGHEST, preferred_element_type=F32) + br_ref[...]
    tm = logits.shape[0]
    lane = lax.broadcasted_iota(jnp.int32, (tm, LANES), 1).astype(F32)
    vals, idxs = [], []
    cur = logits
    for _ in range(TOP_K):
        m = jnp.max(cur, axis=-1, keepdims=True)
        idx = jnp.min(jnp.where(cur == m, lane, float(LANES)), axis=-1, keepdims=True)
        vals.append(m)
        idxs.append(idx)
        cur = jnp.where(lane == idx, -jnp.inf, cur)
    es = [jnp.exp(v - vals[0]) for v in vals]
    den = es[0] + es[1] + es[2] + es[3]
    onehot = jnp.zeros((tm, LANES), F32)
    for idx in idxs:
        onehot = onehot + jnp.where(lane == idx, 1.0, 0.0)
    ti = lax.broadcasted_iota(jnp.int32, (tm, tm), 0)
    tj = lax.broadcasted_iota(jnp.int32, (tm, tm), 1)
    before = jnp.where(ti > tj, 1.0, 0.0).astype(BF16)
    cnt = cnt_ref[...] + jnp.dot(before, onehot.astype(BF16), preferred_element_type=F32)
    route = jnp.zeros((tm, LANES), F32)
    for kk in range(TOP_K):
        rank = jnp.sum(jnp.where(lane == idxs[kk], cnt, 0.0), axis=-1, keepdims=True)
        route = jnp.where(lane == float(kk), es[kk] / den, route)
        route = jnp.where(lane == float(TOP_K + kk), idxs[kk], route)
        route = jnp.where(lane == float(2 * TOP_K + kk), rank, route)
    route_ref[...] = route
    cnt_ref[...] = cnt_ref[...] + jnp.sum(onehot, axis=0, keepdims=True)


def _post_mixer(o_f, o_b, dg, o_na, gab, x, g1, sh2, sc2, dn_norm_w, w_br_a, w_br_b, w_out, norm2_w, wr, br):
    n, d = x.shape
    tm = POST_TM
    per_batch = SEQ // tm
    tok = lambda w: pl.BlockSpec((tm, w), lambda i: (i, 0))
    row = pl.BlockSpec((1, 1, d), lambda i: (i // per_batch, 0, 0))
    const = lambda a: pl.BlockSpec(a.shape, lambda i: (0,) * a.ndim)
    return pl.pallas_call(
        _post_kernel, grid=(n // tm,),
        in_specs=[tok(DN_WIDTH), tok(DN_WIDTH), tok(DN_WIDTH), tok(NA_WIDTH), tok(2 * D_MODEL), tok(d),
                  row, row, row, const(dn_norm_w), const(w_br_a), const(w_br_b), const(w_out),
                  const(norm2_w), const(wr), const(br)],
        out_specs=[tok(d), pl.BlockSpec((tm * ROW_TILE, LANES), lambda i: (i, 0)), tok(LANES),
                   pl.BlockSpec((1, LANES), lambda i: (0, 0))],
        out_shape=[jax.ShapeDtypeStruct((n, d), F32), jax.ShapeDtypeStruct((n * ROW_TILE, LANES), F32),
                   jax.ShapeDtypeStruct((n, LANES), F32), jax.ShapeDtypeStruct((1, LANES), F32)],
        compiler_params=_params(("arbitrary",), VMEM_LIMIT), name="post_mixer_router",
    )(o_f, o_b, dg, o_na, gab, x, g1, sh2, sc2, dn_norm_w, w_br_a, w_br_b, w_out, norm2_w, wr, br)


DMA_ISSUE_UNROLL = 8
DMA_WAIT_UNROLL = 32


def _row_copy(src_hbm, dst_vmem, src_row, dst_row, sem):
    src = pl.ds(pl.multiple_of(src_row * ROW_TILE, ROW_TILE), ROW_TILE)
    dst = pl.ds(pl.multiple_of(dst_row * ROW_TILE, ROW_TILE), ROW_TILE)
    return pltpu.make_async_copy(src_hbm.at[src], dst_vmem.at[dst], sem)


def _expert_kernel(blk_e_ref, nact_ref, tok_ref, h_hbm, w1_ref, b1_ref, w2_ref, b2_ref, y_ref,
                   xbuf, w1b, w2b, sems):
    i = pl.program_id(0)
    n_active = nact_ref[0]
    bm = MOE_BM

    def issue(blk, slot):
        def body(g, carry):
            for j in range(DMA_ISSUE_UNROLL):
                s = g * DMA_ISSUE_UNROLL + j
                _row_copy(h_hbm, xbuf.at[slot], tok_ref[blk * bm + s], s, sems.at[slot]).start()
            return carry
        lax.fori_loop(0, bm // DMA_ISSUE_UNROLL, body, 0)

    def wait(slot):
        def body(g, carry):
            for j in range(DMA_WAIT_UNROLL):
                _row_copy(h_hbm, xbuf.at[slot], 0, g * DMA_WAIT_UNROLL + j, sems.at[slot]).wait()
            return carry
        lax.fori_loop(0, bm // DMA_WAIT_UNROLL, body, 0)

    @pl.when(i == 0)
    def _():
        issue(0, 0)

    @pl.when(i + 1 < n_active)
    def _():
        issue(i + 1, (i + 1) % 2)

    changed = jnp.logical_or(i == 0, blk_e_ref[i] != blk_e_ref[jnp.maximum(i - 1, 0)])

    @pl.when(jnp.logical_and(i < n_active, changed))
    def _():
        w1b[...] = w1_ref[0].astype(BF16)
        w2b[...] = w2_ref[0].astype(BF16)

    @pl.when(i < n_active)
    def _():
        slot = i % 2
        wait(slot)
        xb = _load_row_tiles(xbuf.at[slot], 0, bm).astype(BF16)
        hb = jnp.dot(xb, w1b[...], preferred_element_type=F32) + b1_ref[0]
        gate = jnp.minimum(hb[:, :D_EXPERT], SWIGLU_LIMIT)
        up = jnp.clip(hb[:, D_EXPERT:], -SWIGLU_LIMIT, SWIGLU_LIMIT)
        act = (up + 1.0) * gate * jax.nn.sigmoid(SWIGLU_ALPHA * gate)
        _store_row_tiles(y_ref, jnp.dot(act.astype(BF16), w2b[...], preferred_element_type=F32) + b2_ref[0])

    @pl.when(i >= n_active)
    def _():
        y_ref[...] = jnp.zeros_like(y_ref)


def _expert_ffn(blk_expert, n_active, tok_pad, h2, w1, b1, w2, b2):
    d = w1.shape[1]
    f2 = w1.shape[2]

    def live(i, nact):
        return jnp.minimum(i, jnp.maximum(nact[0] - 1, 0))

    grid_spec = pltpu.PrefetchScalarGridSpec(
        num_scalar_prefetch=3, grid=(MOE_NBLK,),
        in_specs=[pl.BlockSpec(memory_space=pl.ANY),
                  pl.BlockSpec((1, d, f2), lambda i, be, na, tk: (be[live(i, na)], 0, 0)),
                  pl.BlockSpec((1, 1, f2), lambda i, be, na, tk: (be[live(i, na)], 0, 0)),
                  pl.BlockSpec((1, f2 // 2, d), lambda i, be, na, tk: (be[live(i, na)], 0, 0)),
                  pl.BlockSpec((1, 1, d), lambda i, be, na, tk: (be[live(i, na)], 0, 0))],
        out_specs=pl.BlockSpec((MOE_BM * ROW_TILE, LANES), lambda i, be, na, tk: (i, 0)),
        scratch_shapes=[pltpu.VMEM((2, MOE_BM * ROW_TILE, LANES), F32), pltpu.VMEM((d, f2), BF16),
                        pltpu.VMEM((f2 // 2, d), BF16), pltpu.SemaphoreType.DMA((2,))])
    return pl.pallas_call(
        _expert_kernel, grid_spec=grid_spec,
        out_shape=jax.ShapeDtypeStruct((MOE_MPAD * ROW_TILE, LANES), F32),
        compiler_params=_params(("arbitrary",), VMEM_LIMIT), name="expert_ffn",
    )(blk_expert, n_active, tok_pad, h2, w1, b1, w2, b2)


COMB_TM = 128


def _combine_kernel(dest_ref, y_hbm, xl_ref, route_ref, g2_ref, fw_ref, o_ref, ybuf, sems):
    i = pl.program_id(0)
    n_steps = pl.num_programs(0)
    tm = COMB_TM

    def issue(step, slot):
        def body(g, carry):
            for j in range(DMA_ISSUE_UNROLL // TOP_K):
                t = g * (DMA_ISSUE_UNROLL // TOP_K) + j
                for kk in range(TOP_K):
                    _row_copy(y_hbm, ybuf.at[slot], dest_ref[(step * tm + t) * TOP_K + kk], kk * tm + t,
                              sems.at[slot]).start()
            return carry
        lax.fori_loop(0, tm * TOP_K // DMA_ISSUE_UNROLL, body, 0)

    def wait(slot):
        def body(g, carry):
            for j in range(DMA_WAIT_UNROLL):
                _row_copy(y_hbm, ybuf.at[slot], 0, g * DMA_WAIT_UNROLL + j, sems.at[slot]).wait()
            return carry
        lax.fori_loop(0, tm * TOP_K // DMA_WAIT_UNROLL, body, 0)

    @pl.when(i == 0)
    def _():
        issue(0, 0)

    @pl.when(i + 1 < n_steps)
    def _():
        issue(i + 1, (i + 1) % 2)

    slot = i % 2
    wait(slot)
    route = route_ref[...]
    moe = jnp.zeros((tm, D_MODEL), F32)
    for kk in range(TOP_K):
        moe = moe + route[:, kk:kk + 1] * _load_row_tiles(ybuf.at[slot], kk * tm, tm)
    xo = xl_ref[...] + g2_ref[0] * moe
    ms = jnp.mean(xo * xo, axis=-1, keepdims=True)
    o_ref[...] = xo * lax.rsqrt(ms + NORM_EPS) * fw_ref[...]


def _combine(dest, y_sorted, xl, route, g2, final_w):
    n, d = xl.shape
    tm = COMB_TM
    per_batch = SEQ // tm
    grid_spec = pltpu.PrefetchScalarGridSpec(
        num_scalar_prefetch=1, grid=(n // tm,),
        in_specs=[pl.BlockSpec(memory_space=pl.ANY),
                  pl.BlockSpec((tm, d), lambda i, ds: (i, 0)),
                  pl.BlockSpec((tm, LANES), lambda i, ds: (i, 0)),
                  pl.BlockSpec((1, 1, d), lambda i, ds: (i // per_batch, 0, 0)),
                  pl.BlockSpec((1, d), lambda i, ds: (0, 0))],
        out_specs=pl.BlockSpec((tm, d), lambda i, ds: (i, 0)),
        scratch_shapes=[pltpu.VMEM((2, TOP_K * tm * ROW_TILE, LANES), F32), pltpu.SemaphoreType.DMA((2,))])
    return pl.pallas_call(
        _combine_kernel, grid_spec=grid_spec,
        out_shape=jax.ShapeDtypeStruct((n, d), F32),
        compiler_params=_params(("arbitrary",), VMEM_LIMIT), name="moe_combine",
    )(dest, y_sorted, xl, route, g2, final_w)


def _split_in_weights(w_in):
    o = np.cumsum((0, NA_WIDTH, NA_WIDTH, NA_WIDTH, 3 * DN_WIDTH, DN_WIDTH, 2 * DN_HEADS, 2 * DN_HEADS,
                   D_MODEL, D_MODEL))
    wb = w_in.astype(BF16)
    w_na = wb[:, o[0]:o[3]]
    w_dq = wb[:, o[3]:o[4]]
    w_dg = wb[:, o[4]:o[5]]
    w_gab = wb[:, o[7]:o[9]]
    w_dba = jnp.pad(wb[:, o[5]:o[7]], ((0, 0), (0, LANES - 4 * DN_HEADS)))
    return w_na, w_dq, w_dg, w_gab, w_dba


def kernel(x, c, ctx, c_ctx, w_mod, b_mod, norm1_w, w_in, na_rpb, dn_conv_w, dn_a_log, dn_dt_bias, dn_norm_w,
           w_br_a, w_br_b, w_out, norm2_w, w_router, b_router, w1, b1, w2, b2, final_norm_w):
    d = D_MODEL
    cc = jnp.concatenate([c, c_ctx[None], jnp.zeros((16 - BATCH - 1, d), F32)], axis=0)
    mod = _modulation(cc, w_mod[0], b_mod[0][None])
    mod_l = mod[:BATCH].reshape(BATCH, 6, 1, d)
    sh1, sc1, g1, sh2, sc2, g2 = (mod_l[:, i] for i in range(6))
    mod_c = jnp.broadcast_to(mod[BATCH].reshape(6, 1, 1, d), (6, BATCH, 1, d))

    weights = _split_in_weights(w_in[0])
    n1w = norm1_w[0][None]
    naq, nak, nav, dqkv_l, dg_l, gab_l, dba_l = _in_projection(x, sh1, sc1, n1w, weights, 512)
    _, nak_c, nav_c, dqkv_c, _, _, dba_c = _in_projection(ctx, mod_c[0], mod_c[1], n1w, weights, CTX_LEN)

    o_na = _neighborhood_attention(naq, nak, nav, nak_c, nav_c, _na_bias_table(na_rpb[0]))

    cos, sin = _rope_tables(SEQ)
    vk, qp, kt = _dn_prepare(dqkv_c, dqkv_l, dn_conv_w[0], cos, sin)
    pad16 = lambda v: jnp.pad(v.reshape(1, 2 * DN_HEADS), ((0, 0), (2 * DN_HEADS, LANES - 4 * DN_HEADS)))
    g_cols, g_rows = _dn_gates(dba_c, dba_l, pad16(-jnp.exp(dn_a_log[0])), pad16(dn_dt_bias[0]),
                               _gate_select_matrices())
    o_f, o_b = _dn_scan(vk, qp, kt, g_cols, g_rows)

    flat = lambda a: a.reshape(N_TOK, a.shape[-1])
    wr = jnp.pad(w_router[0], ((0, 0), (0, LANES - N_EXPERTS)))
    br = jnp.concatenate([b_router[0], jnp.full((LANES - N_EXPERTS,), NEG_INF, F32)])[None]
    dnw = jnp.tile(dn_norm_w[0], DN_HEADS)[None]
    xl, h2, route, counts = _post_mixer(
        flat(o_f), flat(o_b), flat(dg_l), flat(o_na), flat(gab_l), flat(x), g1, sh2, sc2, dnw,
        w_br_a[0].astype(BF16), w_br_b[0].astype(BF16), w_out[0].astype(BF16), norm2_w[0][None], wr, br)

    e_idx = route[:, TOP_K:2 * TOP_K].astype(jnp.int32)
    rank = route[:, 2 * TOP_K:3 * TOP_K].astype(jnp.int32)
    cnt = counts[0, :N_EXPERTS].astype(jnp.int32)
    padded = (cnt + MOE_BM - 1) // MOE_BM * MOE_BM
    pad_ends = jnp.cumsum(padded)
    pad_starts = pad_ends - padded
    dest = (pad_starts[e_idx] + rank).reshape(-1)
    tok_pad = jnp.zeros((MOE_MPAD,), jnp.int32).at[dest].set(jnp.arange(N_ASG, dtype=jnp.int32) // TOP_K)
    blk_first_row = jnp.arange(MOE_NBLK, dtype=jnp.int32) * MOE_BM
    blk_expert = jnp.minimum(jnp.sum((pad_ends[None, :] <= blk_first_row[:, None]).astype(jnp.int32), axis=1),
                             N_EXPERTS - 1)
    n_active = (pad_ends[-1:] // MOE_BM).astype(jnp.int32)

    y_sorted = _expert_ffn(blk_expert, n_active, tok_pad, h2, w1[0], b1[0][:, None], w2[0], b2[0][:, None])
    out = _combine(dest, y_sorted, xl, route, g2, final_norm_w[None])
    return out.reshape(BATCH, SEQ, d)
```

```python
import functools

import numpy as np
import jax
import jax.numpy as jnp
from jax import lax
from jax.experimental import pallas as pl
from jax.experimental.pallas import tpu as pltpu

F32 = jnp.float32
BF16 = jnp.bfloat16
HIGHEST = lax.Precision.HIGHEST

D_MODEL = 1024
BATCH = 8
SEQ = 2048
GRID_W = 64
GRID_ROWS = SEQ // GRID_W
CTX_LEN = 256
NA_HEADS = 8
NA_HEAD_DIM = 64
NA_KH = 8
NA_KW = 16
NA_SCALE = NA_HEAD_DIM ** -0.5
NA_WIDTH = NA_HEADS * NA_HEAD_DIM
DN_HEADS = 8
DN_HEAD_DIM = 64
DN_WIDTH = DN_HEADS * DN_HEAD_DIM
DN_CONV = 5
DN_CHUNK = 64
ROPE_THETA = 10000.0
N_EXPERTS = 32
TOP_K = 4
D_EXPERT = 1024
SWIGLU_LIMIT = 7.0
SWIGLU_ALPHA = 1.702
NORM_EPS = 1e-6
NEG_INF = -1e30

LANES = 128
N_TOK = BATCH * SEQ
N_ASG = N_TOK * TOP_K
MOE_BM = 256
MOE_MPAD = -(-(N_ASG + N_EXPERTS * (MOE_BM - 1)) // MOE_BM) * MOE_BM
MOE_NBLK = MOE_MPAD // MOE_BM
VMEM_LIMIT = 56 * 1024 * 1024

DN_T = CTX_LEN + SEQ
DN_NCHUNK = DN_T // DN_CHUNK
DN_CTX_CHUNKS = CTX_LEN // DN_CHUNK
DN_PAIRS = DN_HEADS // 2
DN_PAIRS_PER_STEP = 3
DN_INV_BASE = 8


def _params(sem, vmem=None):
    return pltpu.CompilerParams(dimension_semantics=sem, vmem_limit_bytes=vmem)


def _split_bf16(x):
    hi = x.astype(BF16)
    return hi, (x - hi.astype(F32)).astype(BF16)


def _dot_split_lhs(x, w_bf16):
    hi, lo = _split_bf16(x)
    return (jnp.dot(hi, w_bf16, preferred_element_type=F32) + jnp.dot(lo, w_bf16, preferred_element_type=F32))


def _mod_kernel(c_ref, w_ref, b_ref, o_ref):
    c = c_ref[...]
    s = c * jax.nn.sigmoid(c)
    o_ref[...] = jnp.dot(s, w_ref[...], precision=HIGHEST, preferred_element_type=F32) + b_ref[...]


def _modulation(cc, w_mod, b_mod):
    rows, d = cc.shape
    n = w_mod.shape[1]
    bn = 1024
    return pl.pallas_call(
        _mod_kernel, grid=(n // bn,),
        in_specs=[pl.BlockSpec((rows, d), lambda j: (0, 0)),
                  pl.BlockSpec((d, bn), lambda j: (0, j)),
                  pl.BlockSpec((1, bn), lambda j: (0, j))],
        out_specs=pl.BlockSpec((rows, bn), lambda j: (0, j)),
        out_shape=jax.ShapeDtypeStruct((rows, n), F32),
        compiler_params=_params(("parallel",)), name="modulation")(cc, w_mod, b_mod)


def _inproj_kernel(x_ref, sh_ref, sc_ref, nw_ref, wna_ref, wdq_ref, wdg_ref, wgab_ref, wdba_ref,
                   q_ref, k_ref, v_ref, dq_ref, dg_ref, gab_ref, dba_ref):
    x = x_ref[0]
    ms = jnp.mean(x * x, axis=-1, keepdims=True)
    y = x * lax.rsqrt(ms + NORM_EPS) * nw_ref[...]
    h = (y * (1.0 + sc_ref[0]) + sh_ref[0]).astype(BF16)
    na = jnp.dot(h, wna_ref[...], preferred_element_type=F32)
    q_ref[0] = (na[:, :NA_WIDTH] * NA_SCALE).astype(BF16)
    k_ref[0] = na[:, NA_WIDTH:2 * NA_WIDTH].astype(BF16)
    v_ref[0] = na[:, 2 * NA_WIDTH:].astype(BF16)
    dq_ref[0] = jnp.dot(h, wdq_ref[...], preferred_element_type=F32)
    dg_ref[0] = jnp.dot(h, wdg_ref[...], preferred_element_type=F32)
    gab_ref[0] = jnp.dot(h, wgab_ref[...], preferred_element_type=F32)
    dba_ref[0] = jnp.dot(h, wdba_ref[...], preferred_element_type=F32)


def _in_projection(x, shift, scale, norm_w, weights, tm):
    bx, tx, d = x.shape
    wna, wdq, wdg, wgab, wdba = weights
    tok = lambda w: pl.BlockSpec((1, tm, w), lambda b, i: (b, i, 0))
    row = pl.BlockSpec((1, 1, d), lambda b, i: (b, 0, 0))
    const = lambda a: pl.BlockSpec(a.shape, lambda b, i: (0,) * a.ndim)
    widths = (NA_WIDTH, NA_WIDTH, NA_WIDTH, 3 * DN_WIDTH, DN_WIDTH, 2 * D_MODEL, LANES)
    dtypes = (BF16, BF16, BF16, F32, F32, F32, F32)
    return pl.pallas_call(
        _inproj_kernel, grid=(bx, tx // tm),
        in_specs=[tok(d), row, row, const(norm_w), const(wna), const(wdq), const(wdg), const(wgab), const(wdba)],
        out_specs=[tok(w) for w in widths],
        out_shape=[jax.ShapeDtypeStruct((bx, tx, w), dt) for w, dt in zip(widths, dtypes)],
        compiler_params=_params(("parallel", "parallel"), VMEM_LIMIT), name="in_projection",
    )(x, shift, scale, norm_w, wna, wdq, wdg, wgab, wdba)


def _na_bias_table(rpb):
    col = np.arange(GRID_W)
    col_start = np.clip(col - NA_KW // 2, 0, GRID_W - NA_KW)
    mask = (col[None, :] >= col_start[:, None]) & (col[None, :] < col_start[:, None] + NA_KW)
    dc = np.clip(col[None, :] - col[:, None] + NA_KW - 1, 0, 2 * NA_KW - 2)
    pick = np.zeros((2 * NA_KW - 1, GRID_W * GRID_W), np.float32)
    pick[dc.reshape(-1), np.arange(GRID_W * GRID_W)] = 1.0
    n_dr = 2 * NA_KH - 1
    by_dr = jnp.dot(rpb.astype(F32).reshape(NA_HEADS * n_dr, 2 * NA_KW - 1), pick, precision=HIGHEST)
    by_dr = jnp.where(mask[None, None], by_dr.reshape(NA_HEADS, n_dr, GRID_W, GRID_W), NEG_INF)
    tbl = jnp.stack([by_dr[:, c:c + NA_KH] for c in range(NA_KH)], axis=0)
    return tbl.transpose(0, 1, 3, 2, 4).reshape(NA_KH, NA_HEADS, GRID_W, NA_KH * GRID_W)


NA_HEAD_GROUP = 4


def _na_first_row(r):
    return jnp.clip(r - NA_KH // 2, 0, GRID_ROWS - NA_KH)


def _na_kernel(q_ref, k_ref, v_ref, kc_ref, vc_ref, bias_ref, o_ref):
    r = pl.program_id(1)
    start = pl.multiple_of(_na_first_row(r) * GRID_W, GRID_W)
    q = q_ref[0]
    kw = k_ref[0, pl.ds(start, NA_KH * GRID_W), :]
    vw = v_ref[0, pl.ds(start, NA_KH * GRID_W), :]
    kc = kc_ref[0]
    vc = vc_ref[0]
    nt = (((1,), (1,)), ((), ()))
    outs = []
    for g0 in range(0, NA_HEADS, NA_HEAD_GROUP):
        sls = [slice(h * NA_HEAD_DIM, (h + 1) * NA_HEAD_DIM) for h in range(g0, g0 + NA_HEAD_GROUP)]
        s = [lax.dot_general(q[:, sl], kw[:, sl], nt, preferred_element_type=F32) + bias_ref[0, g0 + i]
             for i, sl in enumerate(sls)]
        sc = [lax.dot_general(q[:, sl], kc[:, sl], nt, preferred_element_type=F32) for sl in sls]
        m = [jnp.maximum(jnp.max(a, axis=-1, keepdims=True), jnp.max(b, axis=-1, keepdims=True))
             for a, b in zip(s, sc)]
        p = [jnp.exp(a - mm) for a, mm in zip(s, m)]
        pc = [jnp.exp(b - mm) for b, mm in zip(sc, m)]
        denom = [jnp.sum(a, axis=-1, keepdims=True) + jnp.sum(b, axis=-1, keepdims=True) for a, b in zip(p, pc)]
        o = [jnp.dot(a.astype(BF16), vw[:, sl], preferred_element_type=F32)
             + jnp.dot(b.astype(BF16), vc[:, sl], preferred_element_type=F32) for a, b, sl in zip(p, pc, sls)]
        outs += [oo / dd for oo, dd in zip(o, denom)]
    o_ref[0] = jnp.concatenate(outs, axis=1).astype(BF16)


def _neighborhood_attention(q, k, v, kc, vc, bias):
    b, t, w = q.shape

    def bias_map(bi, r):
        return (_na_first_row(r) - r + NA_KH - 1, 0, 0, 0)

    return pl.pallas_call(
        _na_kernel, grid=(b, GRID_ROWS),
        in_specs=[pl.BlockSpec((1, GRID_W, w), lambda bi, r: (bi, r, 0)),
                  pl.BlockSpec((1, t, w), lambda bi, r: (bi, 0, 0)),
                  pl.BlockSpec((1, t, w), lambda bi, r: (bi, 0, 0)),
                  pl.BlockSpec((1, CTX_LEN, w), lambda bi, r: (bi, 0, 0)),
                  pl.BlockSpec((1, CTX_LEN, w), lambda bi, r: (bi, 0, 0)),
                  pl.BlockSpec((1, NA_HEADS, GRID_W, NA_KH * GRID_W), bias_map)],
        out_specs=pl.BlockSpec((1, GRID_W, w), lambda bi, r: (bi, r, 0)),
        out_shape=jax.ShapeDtypeStruct((b, t, w), BF16),
        compiler_params=_params(("parallel", "parallel"), VMEM_LIMIT), name="neighborhood_attention",
    )(q, k, v, kc, vc, bias)


def _rope_tables(t_len):
    t = jnp.arange(t_len)
    row = (t // GRID_W).astype(F32)
    col = (t % GRID_W).astype(F32)
    n_axis = DN_HEAD_DIM // 4
    freqs = ROPE_THETA ** (-jnp.arange(n_axis, dtype=F32) / n_axis)
    ang = jnp.concatenate([row[:, None] * freqs, col[:, None] * freqs], axis=-1)
    cos = jnp.tile(jnp.cos(ang), (1, 4))
    sin = jnp.sin(ang)
    sin_signed = jnp.tile(jnp.concatenate([-sin, sin], axis=-1), (1, 2))
    return cos, sin_signed


def _conv_silu(x, w):
    tx = x.shape[0]
    t = lax.broadcasted_iota(jnp.int32, x.shape, 0)
    acc = x * w[DN_CONV // 2:DN_CONV // 2 + 1, :]
    for tap in range(DN_CONV):
        shift = DN_CONV // 2 - tap
        if shift == 0:
            continue
        xs = pltpu.roll(x, shift % tx, axis=0)
        valid = (t >= shift) & (t < tx + shift)
        acc = acc + jnp.where(valid, xs, 0.0) * w[tap:tap + 1, :]
    return acc * jax.nn.sigmoid(acc)


def _dnprep_kernel(cq_ref, ck_ref, cv_ref, lq_ref, lk_ref, lv_ref, wq_ref, wk_ref, wv_ref, cos_ref, sin_ref,
                   vk_ref, qp_ref, kt_ref):
    li = lax.broadcasted_iota(jnp.int32, (LANES, LANES), 0) // DN_HEAD_DIM
    lj = lax.broadcasted_iota(jnp.int32, (LANES, LANES), 1) // DN_HEAD_DIM
    same_head = (li == lj).astype(F32)
    half = DN_HEAD_DIM // 2

    def l2norm(y):
        return y * lax.rsqrt(_dot_split_lhs(y * y, same_head.astype(BF16)) + NORM_EPS)

    def rope(y):
        lane = lax.broadcasted_iota(jnp.int32, y.shape, 1)
        partner = jnp.where(lane % DN_HEAD_DIM < half,
                            pltpu.roll(y, LANES - half, axis=1), pltpu.roll(y, half, axis=1))
        return y * cos_ref[...] + partner * sin_ref[...]

    off = 0
    for q_ref, k_ref, v_ref, use_rope in ((cq_ref, ck_ref, cv_ref, False), (lq_ref, lk_ref, lv_ref, True)):
        q = l2norm(_conv_silu(q_ref[0], wq_ref[...]))
        k = l2norm(_conv_silu(k_ref[0], wk_ref[...]))
        v = _conv_silu(v_ref[0], wv_ref[...])
        if use_rope:
            q = rope(q)
            k = rope(k)
        q = q * DN_HEAD_DIM ** -0.5
        tx = q.shape[0]
        rows = slice(off, off + tx)
        left = lax.broadcasted_iota(jnp.int32, q.shape, 1) < DN_HEAD_DIM
        k_sw = pltpu.roll(k, DN_HEAD_DIM, axis=1)
        q_sw = pltpu.roll(q, DN_HEAD_DIM, axis=1)
        vk_ref[0, 0, rows, :] = jnp.where(left, v, k_sw)
        vk_ref[0, 1, rows, :] = jnp.where(left, k_sw, v)
        qp_ref[0, 0, rows, :] = jnp.where(left, 0.0, q_sw)
        qp_ref[0, 1, rows, :] = jnp.where(left, q_sw, 0.0)
        kt_ref[0, 0, :, rows] = k_sw.T
        off += tx


def _dn_prepare(z_c, z_l, conv_w, cos, sin):
    b = z_l.shape[0]
    n_hp = DN_PAIRS

    def part(tx, off):
        return pl.BlockSpec((1, tx, LANES), lambda bi, hp: (bi, 0, off + hp))

    def wpart(off):
        return pl.BlockSpec((DN_CONV, LANES), lambda bi, hp: (0, off + hp))

    tbl = pl.BlockSpec((SEQ, LANES), lambda bi, hp: (0, 0))
    per_head = pl.BlockSpec((1, 2, DN_T, LANES), lambda bi, hp: (bi, hp, 0, 0))
    return pl.pallas_call(
        _dnprep_kernel, grid=(b, n_hp),
        in_specs=[part(CTX_LEN, 0), part(CTX_LEN, n_hp), part(CTX_LEN, 2 * n_hp),
                  part(SEQ, 0), part(SEQ, n_hp), part(SEQ, 2 * n_hp),
                  wpart(0), wpart(n_hp), wpart(2 * n_hp), tbl, tbl],
        out_specs=[per_head, per_head, pl.BlockSpec((1, 1, LANES, DN_T), lambda bi, hp: (bi, hp, 0, 0))],
        out_shape=[jax.ShapeDtypeStruct((b, DN_HEADS, DN_T, LANES), F32),
                   jax.ShapeDtypeStruct((b, DN_HEADS, DN_T, LANES), F32),
                   jax.ShapeDtypeStruct((b, n_hp, LANES, DN_T), F32)],
        compiler_params=_params(("parallel", "parallel"), VMEM_LIMIT), name="dn_prepare",
    )(z_c, z_c, z_c, z_l, z_l, z_l, conv_w, conv_w, conv_w, cos, sin)


def _gate_select_matrices():
    sel = np.zeros((DN_PAIRS, LANES, LANES), np.float32)
    for hp in range(DN_PAIRS):
        for d in range(2):
            for hh in range(2):
                src = d * DN_HEADS + 2 * hp + hh
                sel[hp, src, 2 * d + hh] = 1.0
                sel[hp, 2 * DN_HEADS + src, 4 + 2 * d + hh] = 1.0
    return jnp.asarray(sel)


def _softplus(x):
    return jnp.maximum(x, 0.0) + jnp.log1p(jnp.exp(-jnp.abs(x)))


def _gates_kernel(zc_ref, zl_ref, nega_ref, dtb_ref, sel_ref, gc_ref, gr_ref):
    z = jnp.concatenate([zc_ref[0], zl_ref[0]], axis=0)
    lane = lax.broadcasted_iota(jnp.int32, (DN_CHUNK, LANES), 1)
    ri = lax.broadcasted_iota(jnp.int32, (DN_CHUNK, DN_CHUNK), 0)
    ci = lax.broadcasted_iota(jnp.int32, (DN_CHUNK, DN_CHUNK), 1)
    lower = (ri >= ci).astype(F32)
    upper = (ri <= ci).astype(F32)
    beta = jax.nn.sigmoid(z)
    g = nega_ref[...] * _softplus(z + dtb_ref[...])
    tiles = []
    for c in range(DN_NCHUNK):
        rows = slice(c * DN_CHUNK, (c + 1) * DN_CHUNK)
        gc = g[rows]
        prefix = jnp.dot(lower, gc, precision=HIGHEST, preferred_element_type=F32)
        suffix = jnp.dot(upper, gc, precision=HIGHEST, preferred_element_type=F32)
        tiles.append(jnp.where(lane < 2 * DN_HEADS, beta[rows], jnp.where(lane < 3 * DN_HEADS, prefix, suffix)))
    tile = jnp.concatenate(tiles, axis=0)
    for hp in range(DN_PAIRS):
        cols = jnp.dot(tile, sel_ref[hp], precision=HIGHEST, preferred_element_type=F32)
        gc_ref[0, hp] = cols
        gr_ref[0, hp] = cols.T[0:8, :]


def _dn_gates(z_c, z_l, neg_a, dt_bias, sel):
    b = z_l.shape[0]
    return pl.pallas_call(
        _gates_kernel, grid=(b,),
        in_specs=[pl.BlockSpec((1, CTX_LEN, LANES), lambda bi: (bi, 0, 0)),
                  pl.BlockSpec((1, SEQ, LANES), lambda bi: (bi, 0, 0)),
                  pl.BlockSpec((1, LANES), lambda bi: (0, 0)),
                  pl.BlockSpec((1, LANES), lambda bi: (0, 0)),
                  pl.BlockSpec(sel.shape, lambda bi: (0, 0, 0))],
        out_specs=[pl.BlockSpec((1, DN_PAIRS, DN_T, LANES), lambda bi: (bi, 0, 0, 0)),
                   pl.BlockSpec((1, DN_PAIRS, 8, DN_T), lambda bi: (bi, 0, 0, 0))],
        out_shape=[jax.ShapeDtypeStruct((b, DN_PAIRS, DN_T, LANES), F32),
                   jax.ShapeDtypeStruct((b, DN_PAIRS, 8, DN_T), F32)],
        compiler_params=_params(("parallel",), VMEM_LIMIT), name="dn_gates",
    )(z_c, z_l, neg_a, dt_bias, sel)


def _dot_bf16(a, b):
    return jnp.dot(a.astype(BF16), b.astype(BF16), preferred_element_type=F32)


def _block_masks():
    ri = lax.broadcasted_iota(jnp.int32, (LANES, LANES), 0)
    ci = lax.broadcasted_iota(jnp.int32, (LANES, LANES), 1)
    same = (ri // DN_CHUNK) == (ci // DN_CHUNK)
    rin = ri % DN_CHUNK
    cin = ci % DN_CHUNK
    incl = (same & (rin >= cin), same & (rin <= cin))
    strict = (same & (rin > cin), same & (rin < cin))
    return same, incl, strict


def _scan_phase_a(p, vk_ref, qp_ref, kt_ref, gc_ref, gr_ref, lws_ref, lfin_ref, gl_ref):
    c = DN_CHUNK
    same, incl, strict = _block_masks()
    other = jnp.logical_not(same)
    rows = pl.ds(pl.multiple_of(p * LANES, LANES), LANES)
    vk_a = vk_ref[0, 0, rows, :]
    vk_b = vk_ref[0, 1, rows, :]
    qp_a = qp_ref[0, 0, rows, :]
    qp_b = qp_ref[0, 1, rows, :]
    kts = kt_ref[0, 0, :, rows]
    ktr = pltpu.roll(kts, c, axis=1)
    gct = gc_ref[0, 0, rows, :]
    grt = gr_ref[0, 0, :, rows]
    grr = pltpu.roll(grt, c, axis=1)
    lane8 = lax.broadcasted_iota(jnp.int32, (8, LANES), 1)
    left8 = lane8 < c

    items = []
    for par in range(2):
        hs = slice(par * c, (par + 1) * c)
        vks = jnp.concatenate([vk_a[hs], vk_b[hs]], axis=0)
        qq = jnp.concatenate([qp_a[hs], qp_b[hs]], axis=0)
        kk_in = jnp.where(other, vks, 0.0)
        kk = lax.dot_general(jnp.concatenate([kk_in, qq], axis=0).astype(BF16), kk_in.astype(BF16),
                             (((1,), (1,)), ((), ())), preferred_element_type=F32)
        kt_raw = (jnp.concatenate([ktr[:c], kts[c:]], axis=0) if par == 0
                  else jnp.concatenate([kts[:c], ktr[c:]], axis=0))
        for d in range(2):
            bcol = lambda col: jnp.broadcast_to(gct[hs, col:col + 1], (c, LANES))
            beta = jnp.concatenate([bcol(2 * d), bcol(2 * d + 1)], axis=0)
            g_rows = jnp.concatenate([bcol(4 + 2 * d), bcol(5 + 2 * d)], axis=0)
            if par == 0:
                g_lane = jnp.where(left8, grt[4 + 2 * d:5 + 2 * d], grr[5 + 2 * d:6 + 2 * d])
            else:
                g_lane = jnp.where(left8, grr[4 + 2 * d:5 + 2 * d], grt[5 + 2 * d:6 + 2 * d])
            last = par * c + (c - 1 if d == 0 else 0)
            gl_a = jnp.broadcast_to(grt[4 + 2 * d:5 + 2 * d, last:last + 1], (8, LANES))
            gl_b = jnp.broadcast_to(grt[5 + 2 * d:6 + 2 * d, last:last + 1], (8, LANES))
            g_last = jnp.where(left8, gl_a, gl_b)
            g_cols = jnp.broadcast_to(g_lane[0:1], (LANES, LANES))
            decay = jnp.where(incl[d], jnp.exp(jnp.where(incl[d], g_rows - g_cols, 0.0)), 0.0)
            a = jnp.where(strict[d], kk[:LANES] * decay, 0.0) * beta
            qk = kk[LANES:] * decay
            eg = jnp.exp(g_rows)
            x = vks * beta * jnp.where(other, eg, 1.0)
            qe = qq * eg
            kt_s = jnp.where(other, kt_raw * jnp.exp(jnp.broadcast_to(g_last[0:1] - g_lane[0:1], (LANES, LANES))),
                             0.0)
            idx = (2 * p + par) * 2 + d
            lws_ref[idx, LANES:, :] = qe.astype(BF16)
            lfin_ref[idx] = jnp.concatenate([qk, kt_s], axis=0).astype(BF16)
            gl_ref[idx] = jnp.exp(g_last)
            items.append(dict(idx=idx, d=d, a=a, x=x))
    return items


def _scan_solve(items, xs_ref, lws_ref):
    ri = lax.broadcasted_iota(jnp.int32, (LANES, LANES), 0)
    ci = lax.broadcasted_iota(jnp.int32, (LANES, LANES), 1)
    eye = (ri == ci).astype(F32)
    base = DN_INV_BASE

    def off_block(size, d):
        row_second = (ri // size) % 2 == 1
        col_second = (ci // size) % 2 == 1
        inside = (ri // (2 * size)) == (ci // (2 * size))
        return inside & ((row_second & ~col_second) if d == 0 else (~row_second & col_second))

    for it in items:
        it["a8"] = jnp.where((ri // base) == (ci // base), it["a"], 0.0)
        it["t"] = eye - it["a8"]
    for it in items:
        it["p"] = _dot_bf16(it["a8"], it["a8"])
    for it in items:
        r = _dot_bf16(it["p"], jnp.concatenate([it["p"], it["t"]], axis=1))
        it["p"] = r[:, :LANES]
        it["t"] = it["t"] + r[:, LANES:]
    for it in items:
        it["t"] = it["t"] + _dot_bf16(it["p"], it["t"])
    size = base
    while size < DN_CHUNK:
        for it in items:
            it["m"] = _dot_bf16(jnp.where(off_block(size, it["d"]), it["a"], 0.0), it["t"])
        for it in items:
            it["t"] = it["t"] - _dot_bf16(it["t"], it["m"])
        size *= 2
    for it in items:
        it["x"] = _dot_bf16(it["t"], it["x"])
    for it in items:
        xs_ref[it["idx"]] = it["x"]
        lws_ref[it["idx"], :LANES, :] = it["x"].astype(BF16)


def _scan_phase_b(lo, hi, states, xs_ref, lws_ref, lfin_ref, gl_ref, out_refs):
    c = DN_CHUNK
    same, _, _ = _block_masks()
    other = jnp.logical_not(same)
    left = lax.broadcasted_iota(jnp.int32, (c, LANES), 1) < c

    def body(s, carry):
        u_f = s
        u_b = jnp.where(s < DN_CTX_CHUNKS, DN_CTX_CHUNKS - 1 - s, DN_NCHUNK + DN_CTX_CHUNKS - 1 - s)
        us = (u_f, u_b)
        idx = [us[d] * 2 + d for d in range(2)]
        res = [jnp.dot(lws_ref[idx[d]], carry[d].astype(BF16), preferred_element_type=F32) for d in range(2)]
        vn = [(xs_ref[idx[d]] - res[d][:LANES]).astype(BF16) for d in range(2)]
        fin = [jnp.dot(lfin_ref[idx[d]], vn[d], preferred_element_type=F32) for d in range(2)]
        new = []
        for d in range(2):
            if out_refs is not None:
                o = jnp.where(left, fin[d][:c] + res[d][LANES:LANES + c],
                              fin[d][c:LANES] + res[d][LANES + c:])
                out_refs[d][0, pl.ds(pl.multiple_of((us[d] - DN_CTX_CHUNKS) * c, c), c), :] = o
            decay = jnp.broadcast_to(gl_ref[idx[d]][0:1], (LANES, LANES))
            new.append(carry[d] * decay + jnp.where(other, fin[d][LANES:], 0.0))
        return tuple(new)

    return lax.fori_loop(lo, hi, body, states)


def _scan_kernel(vk_ref, qp_ref, kt_ref, gc_ref, gr_ref, of_ref, ob_ref, xs_ref, lws_ref, lfin_ref, gl_ref):
    def phase_a(step, carry):
        items = []
        for j in range(DN_PAIRS_PER_STEP):
            items += _scan_phase_a(step * DN_PAIRS_PER_STEP + j, vk_ref, qp_ref, kt_ref, gc_ref, gr_ref,
                                   lws_ref, lfin_ref, gl_ref)
        _scan_solve(items, xs_ref, lws_ref)
        return carry

    lax.fori_loop(0, DN_T // LANES // DN_PAIRS_PER_STEP, phase_a, 0)
    zero = jnp.zeros((LANES, LANES), F32)
    scratch = (xs_ref, lws_ref, lfin_ref, gl_ref)
    st = _scan_phase_b(0, DN_CTX_CHUNKS, (zero, zero), *scratch, None)
    _scan_phase_b(DN_CTX_CHUNKS, DN_NCHUNK, st, *scratch, (of_ref, ob_ref))


def _dn_scan(vk, qp, kt, g_cols, g_rows):
    b = vk.shape[0]
    n_items = 2 * DN_NCHUNK
    out = pl.BlockSpec((1, SEQ, LANES), lambda bi, hp: (bi, 0, hp))
    return pl.pallas_call(
        _scan_kernel, grid=(b, DN_PAIRS),
        in_specs=[pl.BlockSpec((1, 2, DN_T, LANES), lambda bi, hp: (bi, hp, 0, 0)),
                  pl.BlockSpec((1, 2, DN_T, LANES), lambda bi, hp: (bi, hp, 0, 0)),
                  pl.BlockSpec((1, 1, LANES, DN_T), lambda bi, hp: (bi, hp, 0, 0)),
                  pl.BlockSpec((1, 1, DN_T, LANES), lambda bi, hp: (bi, hp, 0, 0)),
                  pl.BlockSpec((1, 1, 8, DN_T), lambda bi, hp: (bi, hp, 0, 0))],
        out_specs=[out, out],
        out_shape=[jax.ShapeDtypeStruct((b, SEQ, DN_WIDTH), F32)] * 2,
        scratch_shapes=[pltpu.VMEM((n_items, LANES, LANES), F32),
                        pltpu.VMEM((n_items, 2 * LANES, LANES), BF16),
                        pltpu.VMEM((n_items, 2 * LANES, LANES), BF16),
                        pltpu.VMEM((n_items, 8, LANES), F32)],
        compiler_params=_params(("parallel", "parallel"), VMEM_LIMIT), name="dn_scan",
    )(vk, qp, kt, g_cols, g_rows)


POST_TM = 256
ROW_TILE = 8


def _store_row_tiles(ref, val):
    m = val.shape[0]
    for j in range(ROW_TILE):
        ref[pl.ds(j, m, stride=ROW_TILE), :] = val[:, j * LANES:(j + 1) * LANES]


def _load_row_tiles(ref, first, m):
    return jnp.concatenate([ref[pl.ds(first * ROW_TILE + j, m, stride=ROW_TILE), :] for j in range(ROW_TILE)],
                           axis=1)


def _post_kernel(of_ref, ob_ref, dg_ref, ona_ref, gab_ref, x_ref, g1_ref, sh2_ref, sc2_ref,
                 dnw_ref, wa_ref, wb_ref, wo_ref, n2w_ref, wr_ref, br_ref,
                 xl_ref, h2_ref, route_ref, cnt_ref):
    i = pl.program_id(0)

    @pl.when(i == 0)
    def _():
        cnt_ref[...] = jnp.zeros_like(cnt_ref)

    o = of_ref[...] + ob_ref[...]
    hi = lax.broadcasted_iota(jnp.int32, (DN_WIDTH, DN_WIDTH), 0) // DN_HEAD_DIM
    hj = lax.broadcasted_iota(jnp.int32, (DN_WIDTH, DN_WIDTH), 1) // DN_HEAD_DIM
    head_mean = jnp.where(hi == hj, 1.0 / DN_HEAD_DIM, 0.0).astype(BF16)
    ms = _dot_split_lhs(o * o, head_mean)
    dg = dg_ref[...]
    o_dn = (o * lax.rsqrt(ms + NORM_EPS) * dnw_ref[...]) * (dg * jax.nn.sigmoid(dg))
    gab = gab_ref[...]
    ya = jnp.dot(ona_ref[...], wa_ref[...], preferred_element_type=F32)
    yb = jnp.dot(o_dn.astype(BF16), wb_ref[...], preferred_element_type=F32)
    y = jax.nn.sigmoid(gab[:, :D_MODEL]) * ya + jax.nn.sigmoid(gab[:, D_MODEL:]) * yb
    y = jnp.dot(y.astype(BF16), wo_ref[...], preferred_element_type=F32)
    xl = x_ref[...] + g1_ref[0] * y
    xl_ref[...] = xl
    ms2 = jnp.mean(xl * xl, axis=-1, keepdims=True)
    h2 = (xl * lax.rsqrt(ms2 + NORM_EPS) * n2w_ref[...]) * (1.0 + sc2_ref[0]) + sh2_ref[0]
    _store_row_tiles(h2_ref, h2)

    h_hi, h_lo = _split_bf16(h2)
    logits = (jnp.dot(h_hi, wr_ref[0], preferred_element_type=F32) + jnp.dot(h_lo, wr_ref[0], preferred_element_type=F32)
              + jnp.dot(h_hi, wr_ref[1], preferred_element_type=F32) + br_ref[...])
    tm = logits.shape[0]
    lane = lax.broadcasted_iota(jnp.int32, (tm, LANES), 1).astype(F32)
    vals, idxs = [], []
    cur = logits
    for _ in range(TOP_K):
        m = jnp.max(cur, axis=-1, keepdims=True)
        idx = jnp.min(jnp.where(cur == m, lane, float(LANES)), axis=-1, keepdims=True)
        vals.append(m)
        idxs.append(idx)
        cur = jnp.where(lane == idx, -jnp.inf, cur)
    es = [jnp.exp(v - vals[0]) for v in vals]
    den = es[0] + es[1] + es[2] + es[3]
    onehot = jnp.zeros((tm, LANES), F32)
    for idx in idxs:
        onehot = onehot + jnp.where(lane == idx, 1.0, 0.0)
    ti = lax.broadcasted_iota(jnp.int32, (tm, tm), 0)
    tj = lax.broadcasted_iota(jnp.int32, (tm, tm), 1)
    before = jnp.where(ti > tj, 1.0, 0.0).astype(BF16)
    cnt = cnt_ref[...] + jnp.dot(before, onehot.astype(BF16), preferred_element_type=F32)
    route = jnp.zeros((tm, LANES), F32)
    for kk in range(TOP_K):
        rank = jnp.sum(jnp.where(lane == idxs[kk], cnt, 0.0), axis=-1, keepdims=True)
        route = jnp.where(lane == float(kk), es[kk] / den, route)
        route = jnp.where(lane == float(TOP_K + kk), idxs[kk], route)
        route = jnp.where(lane == float(2 * TOP_K + kk), rank, route)
    route_ref[...] = route
    cnt_ref[...] = cnt_ref[...] + jnp.sum(onehot, axis=0, keepdims=True)


def _post_mixer(o_f, o_b, dg, o_na, gab, x, g1, sh2, sc2, dn_norm_w, w_br_a, w_br_b, w_out, norm2_w, wr, br):
    n, d = x.shape
    tm = POST_TM
    per_batch = SEQ // tm
    tok = lambda w: pl.BlockSpec((tm, w), lambda i: (i, 0))
    row = pl.BlockSpec((1, 1, d), lambda i: (i // per_batch, 0, 0))
    const = lambda a: pl.BlockSpec(a.shape, lambda i: (0,) * a.ndim)
    return pl.pallas_call(
        _post_kernel, grid=(n // tm,),
        in_specs=[tok(DN_WIDTH), tok(DN_WIDTH), tok(DN_WIDTH), tok(NA_WIDTH), tok(2 * D_MODEL), tok(d),
                  row, row, row, const(dn_norm_w), const(w_br_a), const(w_br_b), const(w_out),
                  const(norm2_w), const(wr), const(br)],
        out_specs=[tok(d), pl.BlockSpec((tm * ROW_TILE, LANES), lambda i: (i, 0)), tok(LANES),
                   pl.BlockSpec((1, LANES), lambda i: (0, 0))],
        out_shape=[jax.ShapeDtypeStruct((n, d), F32), jax.ShapeDtypeStruct((n * ROW_TILE, LANES), F32),
                   jax.ShapeDtypeStruct((n, LANES), F32), jax.ShapeDtypeStruct((1, LANES), F32)],
        compiler_params=_params(("arbitrary",), VMEM_LIMIT), name="post_mixer_router",
    )(o_f, o_b, dg, o_na, gab, x, g1, sh2, sc2, dn_norm_w, w_br_a, w_br_b, w_out, norm2_w, wr, br)


DMA_ISSUE_UNROLL = 8
DMA_WAIT_UNROLL = 32


def _row_copy(src_hbm, dst_vmem, src_row, dst_row, sem):
    src = pl.ds(pl.multiple_of(src_row * ROW_TILE, ROW_TILE), ROW_TILE)
    dst = pl.ds(pl.multiple_of(dst_row * ROW_TILE, ROW_TILE), ROW_TILE)
    return pltpu.make_async_copy(src_hbm.at[src], dst_vmem.at[dst], sem)


def _expert_kernel(blk_e_ref, nact_ref, tok_ref, h_hbm, w1_ref, b1_ref, w2_ref, b2_ref, y_ref,
                   xbuf, w1b, w2b, sems):
    i = pl.program_id(0)
    n_active = nact_ref[0]
    bm = MOE_BM

    def issue(blk, slot):
        def body(g, carry):
            for j in range(DMA_ISSUE_UNROLL):
                s = g * DMA_ISSUE_UNROLL + j
                _row_copy(h_hbm, xbuf.at[slot], tok_ref[blk * bm + s], s, sems.at[slot]).start()
            return carry
        lax.fori_loop(0, bm // DMA_ISSUE_UNROLL, body, 0)

    def wait(slot):
        def body(g, carry):
            for j in range(DMA_WAIT_UNROLL):
                _row_copy(h_hbm, xbuf.at[slot], 0, g * DMA_WAIT_UNROLL + j, sems.at[slot]).wait()
            return carry
        lax.fori_loop(0, bm // DMA_WAIT_UNROLL, body, 0)

    @pl.when(i == 0)
    def _():
        issue(0, 0)

    @pl.when(i + 1 < n_active)
    def _():
        issue(i + 1, (i + 1) % 2)

    changed = jnp.logical_or(i == 0, blk_e_ref[i] != blk_e_ref[jnp.maximum(i - 1, 0)])

    @pl.when(jnp.logical_and(i < n_active, changed))
    def _():
        w1b[...] = w1_ref[0].astype(BF16)
        w2b[...] = w2_ref[0].astype(BF16)

    @pl.when(i < n_active)
    def _():
        slot = i % 2
        wait(slot)
        xb = _load_row_tiles(xbuf.at[slot], 0, bm).astype(BF16)
        hb = jnp.dot(xb, w1b[...], preferred_element_type=F32) + b1_ref[0]
        gate = jnp.minimum(hb[:, :D_EXPERT], SWIGLU_LIMIT)
        up = jnp.clip(hb[:, D_EXPERT:], -SWIGLU_LIMIT, SWIGLU_LIMIT)
        act = (up + 1.0) * gate * jax.nn.sigmoid(SWIGLU_ALPHA * gate)
        _store_row_tiles(y_ref, jnp.dot(act.astype(BF16), w2b[...], preferred_element_type=F32) + b2_ref[0])

    @pl.when(i >= n_active)
    def _():
        y_ref[...] = jnp.zeros_like(y_ref)


def _expert_ffn(blk_expert, n_active, tok_pad, h2, w1, b1, w2, b2):
    d = w1.shape[1]
    f2 = w1.shape[2]

    def live(i, nact):
        return jnp.minimum(i, jnp.maximum(nact[0] - 1, 0))

    grid_spec = pltpu.PrefetchScalarGridSpec(
        num_scalar_prefetch=3, grid=(MOE_NBLK,),
        in_specs=[pl.BlockSpec(memory_space=pl.ANY),
                  pl.BlockSpec((1, d, f2), lambda i, be, na, tk: (be[live(i, na)], 0, 0)),
                  pl.BlockSpec((1, 1, f2), lambda i, be, na, tk: (be[live(i, na)], 0, 0)),
                  pl.BlockSpec((1, f2 // 2, d), lambda i, be, na, tk: (be[live(i, na)], 0, 0)),
                  pl.BlockSpec((1, 1, d), lambda i, be, na, tk: (be[live(i, na)], 0, 0))],
        out_specs=pl.BlockSpec((MOE_BM * ROW_TILE, LANES), lambda i, be, na, tk: (i, 0)),
        scratch_shapes=[pltpu.VMEM((2, MOE_BM * ROW_TILE, LANES), F32), pltpu.VMEM((d, f2), BF16),
                        pltpu.VMEM((f2 // 2, d), BF16), pltpu.SemaphoreType.DMA((2,))])
    return pl.pallas_call(
        _expert_kernel, grid_spec=grid_spec,
        out_shape=jax.ShapeDtypeStruct((MOE_MPAD * ROW_TILE, LANES), F32),
        compiler_params=_params(("arbitrary",), VMEM_LIMIT), name="expert_ffn",
    )(blk_expert, n_active, tok_pad, h2, w1, b1, w2, b2)


COMB_TM = 128


def _combine_kernel(dest_ref, y_hbm, xl_ref, route_ref, g2_ref, fw_ref, o_ref, ybuf, sems):
    i = pl.program_id(0)
    n_steps = pl.num_programs(0)
    tm = COMB_TM

    def issue(step, slot):
        def body(g, carry):
            for j in range(DMA_ISSUE_UNROLL // TOP_K):
                t = g * (DMA_ISSUE_UNROLL // TOP_K) + j
                for kk in range(TOP_K):
                    _row_copy(y_hbm, ybuf.at[slot], dest_ref[(step * tm + t) * TOP_K + kk], kk * tm + t,
                              sems.at[slot]).start()
            return carry
        lax.fori_loop(0, tm * TOP_K // DMA_ISSUE_UNROLL, body, 0)

    def wait(slot):
        def body(g, carry):
            for j in range(DMA_WAIT_UNROLL):
                _row_copy(y_hbm, ybuf.at[slot], 0, g * DMA_WAIT_UNROLL + j, sems.at[slot]).wait()
            return carry
        lax.fori_loop(0, tm * TOP_K // DMA_WAIT_UNROLL, body, 0)

    @pl.when(i == 0)
    def _():
        issue(0, 0)

    @pl.when(i + 1 < n_steps)
    def _():
        issue(i + 1, (i + 1) % 2)

    slot = i % 2
    wait(slot)
    route = route_ref[...]
    moe = jnp.zeros((tm, D_MODEL), F32)
    for kk in range(TOP_K):
        moe = moe + route[:, kk:kk + 1] * _load_row_tiles(ybuf.at[slot], kk * tm, tm)
    xo = xl_ref[...] + g2_ref[0] * moe
    ms = jnp.mean(xo * xo, axis=-1, keepdims=True)
    o_ref[...] = xo * lax.rsqrt(ms + NORM_EPS) * fw_ref[...]


def _combine(dest, y_sorted, xl, route, g2, final_w):
    n, d = xl.shape
    tm = COMB_TM
    per_batch = SEQ // tm
    grid_spec = pltpu.PrefetchScalarGridSpec(
        num_scalar_prefetch=1, grid=(n // tm,),
        in_specs=[pl.BlockSpec(memory_space=pl.ANY),
                  pl.BlockSpec((tm, d), lambda i, ds: (i, 0)),
                  pl.BlockSpec((tm, LANES), lambda i, ds: (i, 0)),
                  pl.BlockSpec((1, 1, d), lambda i, ds: (i // per_batch, 0, 0)),
                  pl.BlockSpec((1, d), lambda i, ds: (0, 0))],
        out_specs=pl.BlockSpec((tm, d), lambda i, ds: (i, 0)),
        scratch_shapes=[pltpu.VMEM((2, TOP_K * tm * ROW_TILE, LANES), F32), pltpu.SemaphoreType.DMA((2,))])
    return pl.pallas_call(
        _combine_kernel, grid_spec=grid_spec,
        out_shape=jax.ShapeDtypeStruct((n, d), F32),
        compiler_params=_params(("arbitrary",), VMEM_LIMIT), name="moe_combine",
    )(dest, y_sorted, xl, route, g2, final_w)


def _split_in_weights(w_in):
    o = np.cumsum((0, NA_WIDTH, NA_WIDTH, NA_WIDTH, 3 * DN_WIDTH, DN_WIDTH, 2 * DN_HEADS, 2 * DN_HEADS,
                   D_MODEL, D_MODEL))
    wb = w_in.astype(BF16)
    w_na = wb[:, o[0]:o[3]]
    w_dq = wb[:, o[3]:o[4]]
    w_dg = wb[:, o[4]:o[5]]
    w_gab = wb[:, o[7]:o[9]]
    w_dba = jnp.pad(wb[:, o[5]:o[7]], ((0, 0), (0, LANES - 4 * DN_HEADS)))
    return w_na, w_dq, w_dg, w_gab, w_dba


def kernel(x, c, ctx, c_ctx, w_mod, b_mod, norm1_w, w_in, na_rpb, dn_conv_w, dn_a_log, dn_dt_bias, dn_norm_w,
           w_br_a, w_br_b, w_out, norm2_w, w_router, b_router, w1, b1, w2, b2, final_norm_w):
    d = D_MODEL
    cc = jnp.concatenate([c, c_ctx[None], jnp.zeros((16 - BATCH - 1, d), F32)], axis=0)
    mod = _modulation(cc, w_mod[0], b_mod[0][None])
    mod_l = mod[:BATCH].reshape(BATCH, 6, 1, d)
    sh1, sc1, g1, sh2, sc2, g2 = (mod_l[:, i] for i in range(6))
    mod_c = jnp.broadcast_to(mod[BATCH].reshape(6, 1, 1, d), (6, BATCH, 1, d))

    weights = _split_in_weights(w_in[0])
    n1w = norm1_w[0][None]
    naq, nak, nav, dqkv_l, dg_l, gab_l, dba_l = _in_projection(x, sh1, sc1, n1w, weights, 512)
    _, nak_c, nav_c, dqkv_c, _, _, dba_c = _in_projection(ctx, mod_c[0], mod_c[1], n1w, weights, CTX_LEN)

    o_na = _neighborhood_attention(naq, nak, nav, nak_c, nav_c, _na_bias_table(na_rpb[0]))

    cos, sin = _rope_tables(SEQ)
    vk, qp, kt = _dn_prepare(dqkv_c, dqkv_l, dn_conv_w[0], cos, sin)
    pad16 = lambda v: jnp.pad(v.reshape(1, 2 * DN_HEADS), ((0, 0), (2 * DN_HEADS, LANES - 4 * DN_HEADS)))
    g_cols, g_rows = _dn_gates(dba_c, dba_l, pad16(-jnp.exp(dn_a_log[0])), pad16(dn_dt_bias[0]),
                               _gate_select_matrices())
    o_f, o_b = _dn_scan(vk, qp, kt, g_cols, g_rows)

    flat = lambda a: a.reshape(N_TOK, a.shape[-1])
    wr = jnp.stack(_split_bf16(jnp.pad(w_router[0], ((0, 0), (0, LANES - N_EXPERTS)))))
    br = jnp.concatenate([b_router[0], jnp.full((LANES - N_EXPERTS,), NEG_INF, F32)])[None]
    dnw = jnp.tile(dn_norm_w[0], DN_HEADS)[None]
    xl, h2, route, counts = _post_mixer(
        flat(o_f), flat(o_b), flat(dg_l), flat(o_na), flat(gab_l), flat(x), g1, sh2, sc2, dnw,
        w_br_a[0].astype(BF16), w_br_b[0].astype(BF16), w_out[0].astype(BF16), norm2_w[0][None], wr, br)

    e_idx = route[:, TOP_K:2 * TOP_K].astype(jnp.int32)
    rank = route[:, 2 * TOP_K:3 * TOP_K].astype(jnp.int32)
    cnt = counts[0, :N_EXPERTS].astype(jnp.int32)
    padded = (cnt + MOE_BM - 1) // MOE_BM * MOE_BM
    pad_ends = jnp.cumsum(padded)
    pad_starts = pad_ends - padded
    dest = (pad_starts[e_idx] + rank).reshape(-1)
    tok_pad = jnp.zeros((MOE_MPAD,), jnp.int32).at[dest].set(jnp.arange(N_ASG, dtype=jnp.int32) // TOP_K)
    blk_first_row = jnp.arange(MOE_NBLK, dtype=jnp.int32) * MOE_BM
    blk_expert = jnp.minimum(jnp.sum((pad_ends[None, :] <= blk_first_row[:, None]).astype(jnp.int32), axis=1),
                             N_EXPERTS - 1)
    n_active = (pad_ends[-1:] // MOE_BM).astype(jnp.int32)

    y_sorted = _expert_ffn(blk_expert, n_active, tok_pad, h2, w1[0], b1[0][:, None], w2[0], b2[0][:, None])
    out = _combine(dest, y_sorted, xl, route, g2, final_norm_w[None])
    return out.reshape(BATCH, SEQ, d)
```

```python
import functools

import numpy as np
import jax
import jax.numpy as jnp
from jax import lax
from jax.experimental import pallas as pl
from jax.experimental.pallas import tpu as pltpu

F32 = jnp.float32
BF16 = jnp.bfloat16
HIGHEST = lax.Precision.HIGHEST

D_MODEL = 1024
BATCH = 8
SEQ = 2048
GRID_W = 64
GRID_ROWS = SEQ // GRID_W
CTX_LEN = 256
NA_HEADS = 8
NA_HEAD_DIM = 64
NA_KH = 8
NA_KW = 16
NA_SCALE = NA_HEAD_DIM ** -0.5
NA_WIDTH = NA_HEADS * NA_HEAD_DIM
DN_HEADS = 8
DN_HEAD_DIM = 64
DN_WIDTH = DN_HEADS * DN_HEAD_DIM
DN_CONV = 5
DN_CHUNK = 64
ROPE_THETA = 10000.0
N_EXPERTS = 32
TOP_K = 4
D_EXPERT = 1024
SWIGLU_LIMIT = 7.0
SWIGLU_ALPHA = 1.702
NORM_EPS = 1e-6
NEG_INF = -1e30

LANES = 128
N_TOK = BATCH * SEQ
N_ASG = N_TOK * TOP_K
MOE_BM = 256
MOE_MPAD = -(-(N_ASG + N_EXPERTS * (MOE_BM - 1)) // MOE_BM) * MOE_BM
MOE_NBLK = MOE_MPAD // MOE_BM
VMEM_LIMIT = 56 * 1024 * 1024

DN_T = CTX_LEN + SEQ
DN_NCHUNK = DN_T // DN_CHUNK
DN_CTX_CHUNKS = CTX_LEN // DN_CHUNK
DN_PAIRS = DN_HEADS // 2
DN_PAIRS_PER_STEP = 3
DN_INV_BASE = 8


def _params(sem, vmem=None):
    return pltpu.CompilerParams(dimension_semantics=sem, vmem_limit_bytes=vmem)


def _split_bf16(x):
    hi = x.astype(BF16)
    return hi, (x - hi.astype(F32)).astype(BF16)


def _dot_split_lhs(x, w_bf16):
    hi, lo = _split_bf16(x)
    return (jnp.dot(hi, w_bf16, preferred_element_type=F32) + jnp.dot(lo, w_bf16, preferred_element_type=F32))


def _mod_kernel(c_ref, w_ref, b_ref, o_ref):
    c = c_ref[...]
    s = c * jax.nn.sigmoid(c)
    o_ref[...] = jnp.dot(s, w_ref[...], precision=HIGHEST, preferred_element_type=F32) + b_ref[...]


def _modulation(cc, w_mod, b_mod):
    rows, d = cc.shape
    n = w_mod.shape[1]
    bn = 1024
    return pl.pallas_call(
        _mod_kernel, grid=(n // bn,),
        in_specs=[pl.BlockSpec((rows, d), lambda j: (0, 0)),
                  pl.BlockSpec((d, bn), lambda j: (0, j)),
                  pl.BlockSpec((1, bn), lambda j: (0, j))],
        out_specs=pl.BlockSpec((rows, bn), lambda j: (0, j)),
        out_shape=jax.ShapeDtypeStruct((rows, n), F32),
        compiler_params=_params(("parallel",)), name="modulation")(cc, w_mod, b_mod)


def _inproj_kernel(x_ref, sh_ref, sc_ref, nw_ref, wna_ref, wdq_ref, wdg_ref, wgab_ref, wdba_ref,
                   q_ref, k_ref, v_ref, dq_ref, dg_ref, gab_ref, dba_ref):
    x = x_ref[0]
    ms = jnp.mean(x * x, axis=-1, keepdims=True)
    y = x * lax.rsqrt(ms + NORM_EPS) * nw_ref[...]
    h = (y * (1.0 + sc_ref[0]) + sh_ref[0]).astype(BF16)
    na = jnp.dot(h, wna_ref[...], preferred_element_type=F32)
    q_ref[0] = (na[:, :NA_WIDTH] * NA_SCALE).astype(BF16)
    k_ref[0] = na[:, NA_WIDTH:2 * NA_WIDTH].astype(BF16)
    v_ref[0] = na[:, 2 * NA_WIDTH:].astype(BF16)
    dq_ref[0] = jnp.dot(h, wdq_ref[...], preferred_element_type=F32)
    dg_ref[0] = jnp.dot(h, wdg_ref[...], preferred_element_type=F32)
    gab_ref[0] = jnp.dot(h, wgab_ref[...], preferred_element_type=F32)
    dba_ref[0] = jnp.dot(h, wdba_ref[...], preferred_element_type=F32)


def _in_projection(x, shift, scale, norm_w, weights, tm):
    bx, tx, d = x.shape
    wna, wdq, wdg, wgab, wdba = weights
    tok = lambda w: pl.BlockSpec((1, tm, w), lambda b, i: (b, i, 0))
    row = pl.BlockSpec((1, 1, d), lambda b, i: (b, 0, 0))
    const = lambda a: pl.BlockSpec(a.shape, lambda b, i: (0,) * a.ndim)
    widths = (NA_WIDTH, NA_WIDTH, NA_WIDTH, 3 * DN_WIDTH, DN_WIDTH, 2 * D_MODEL, LANES)
    dtypes = (BF16, BF16, BF16, F32, F32, F32, F32)
    return pl.pallas_call(
        _inproj_kernel, grid=(bx, tx // tm),
        in_specs=[tok(d), row, row, const(norm_w), const(wna), const(wdq), const(wdg), const(wgab), const(wdba)],
        out_specs=[tok(w) for w in widths],
        out_shape=[jax.ShapeDtypeStruct((bx, tx, w), dt) for w, dt in zip(widths, dtypes)],
        compiler_params=_params(("parallel", "parallel"), VMEM_LIMIT), name="in_projection",
    )(x, shift, scale, norm_w, wna, wdq, wdg, wgab, wdba)


def _na_bias_table(rpb):
    col = np.arange(GRID_W)
    col_start = np.clip(col - NA_KW // 2, 0, GRID_W - NA_KW)
    mask = (col[None, :] >= col_start[:, None]) & (col[None, :] < col_start[:, None] + NA_KW)
    dc = np.clip(col[None, :] - col[:, None] + NA_KW - 1, 0, 2 * NA_KW - 2)
    pick = np.zeros((2 * NA_KW - 1, GRID_W * GRID_W), np.float32)
    pick[dc.reshape(-1), np.arange(GRID_W * GRID_W)] = 1.0
    n_dr = 2 * NA_KH - 1
    by_dr = jnp.dot(rpb.astype(F32).reshape(NA_HEADS * n_dr, 2 * NA_KW - 1), pick, precision=HIGHEST)
    by_dr = jnp.where(mask[None, None], by_dr.reshape(NA_HEADS, n_dr, GRID_W, GRID_W), NEG_INF)
    tbl = jnp.stack([by_dr[:, c:c + NA_KH] for c in range(NA_KH)], axis=0)
    return tbl.transpose(0, 1, 3, 2, 4).reshape(NA_KH, NA_HEADS, GRID_W, NA_KH * GRID_W)


NA_HEAD_GROUP = 4


def _na_first_row(r):
    return jnp.clip(r - NA_KH // 2, 0, GRID_ROWS - NA_KH)


def _na_kernel(q_ref, k_ref, v_ref, kc_ref, vc_ref, bias_ref, o_ref):
    r = pl.program_id(1)
    start = pl.multiple_of(_na_first_row(r) * GRID_W, GRID_W)
    q = q_ref[0]
    kw = k_ref[0, pl.ds(start, NA_KH * GRID_W), :]
    vw = v_ref[0, pl.ds(start, NA_KH * GRID_W), :]
    kc = kc_ref[0]
    vc = vc_ref[0]
    nt = (((1,), (1,)), ((), ()))
    outs = []
    for g0 in range(0, NA_HEADS, NA_HEAD_GROUP):
        sls = [slice(h * NA_HEAD_DIM, (h + 1) * NA_HEAD_DIM) for h in range(g0, g0 + NA_HEAD_GROUP)]
        s = [lax.dot_general(q[:, sl], kw[:, sl], nt, preferred_element_type=F32) + bias_ref[0, g0 + i]
             for i, sl in enumerate(sls)]
        sc = [lax.dot_general(q[:, sl], kc[:, sl], nt, preferred_element_type=F32) for sl in sls]
        m = [jnp.maximum(jnp.max(a, axis=-1, keepdims=True), jnp.max(b, axis=-1, keepdims=True))
             for a, b in zip(s, sc)]
        p = [jnp.exp(a - mm) for a, mm in zip(s, m)]
        pc = [jnp.exp(b - mm) for b, mm in zip(sc, m)]
        denom = [jnp.sum(a, axis=-1, keepdims=True) + jnp.sum(b, axis=-1, keepdims=True) for a, b in zip(p, pc)]
        o = [jnp.dot(a.astype(BF16), vw[:, sl], preferred_element_type=F32)
             + jnp.dot(b.astype(BF16), vc[:, sl], preferred_element_type=F32) for a, b, sl in zip(p, pc, sls)]
        outs += [oo / dd for oo, dd in zip(o, denom)]
    o_ref[0] = jnp.concatenate(outs, axis=1).astype(BF16)


def _neighborhood_attention(q, k, v, kc, vc, bias):
    b, t, w = q.shape

    def bias_map(bi, r):
        return (_na_first_row(r) - r + NA_KH - 1, 0, 0, 0)

    return pl.pallas_call(
        _na_kernel, grid=(b, GRID_ROWS),
        in_specs=[pl.BlockSpec((1, GRID_W, w), lambda bi, r: (bi, r, 0)),
                  pl.BlockSpec((1, t, w), lambda bi, r: (bi, 0, 0)),
                  pl.BlockSpec((1, t, w), lambda bi, r: (bi, 0, 0)),
                  pl.BlockSpec((1, CTX_LEN, w), lambda bi, r: (bi, 0, 0)),
                  pl.BlockSpec((1, CTX_LEN, w), lambda bi, r: (bi, 0, 0)),
                  pl.BlockSpec((1, NA_HEADS, GRID_W, NA_KH * GRID_W), bias_map)],
        out_specs=pl.BlockSpec((1, GRID_W, w), lambda bi, r: (bi, r, 0)),
        out_shape=jax.ShapeDtypeStruct((b, t, w), BF16),
        compiler_params=_params(("parallel", "parallel"), VMEM_LIMIT), name="neighborhood_attention",
    )(q, k, v, kc, vc, bias)


def _rope_tables(t_len):
    t = jnp.arange(t_len)
    row = (t // GRID_W).astype(F32)
    col = (t % GRID_W).astype(F32)
    n_axis = DN_HEAD_DIM // 4
    freqs = ROPE_THETA ** (-jnp.arange(n_axis, dtype=F32) / n_axis)
    ang = jnp.concatenate([row[:, None] * freqs, col[:, None] * freqs], axis=-1)
    cos = jnp.tile(jnp.cos(ang), (1, 4))
    sin = jnp.sin(ang)
    sin_signed = jnp.tile(jnp.concatenate([-sin, sin], axis=-1), (1, 2))
    return cos, sin_signed


def _conv_silu(x, w):
    tx = x.shape[0]
    t = lax.broadcasted_iota(jnp.int32, x.shape, 0)
    acc = x * w[DN_CONV // 2:DN_CONV // 2 + 1, :]
    for tap in range(DN_CONV):
        shift = DN_CONV // 2 - tap
        if shift == 0:
            continue
        xs = pltpu.roll(x, shift % tx, axis=0)
        valid = (t >= shift) & (t < tx + shift)
        acc = acc + jnp.where(valid, xs, 0.0) * w[tap:tap + 1, :]
    return acc * jax.nn.sigmoid(acc)


def _dnprep_kernel(cq_ref, ck_ref, cv_ref, lq_ref, lk_ref, lv_ref, wq_ref, wk_ref, wv_ref, cos_ref, sin_ref,
                   vk_ref, qp_ref, kt_ref):
    li = lax.broadcasted_iota(jnp.int32, (LANES, LANES), 0) // DN_HEAD_DIM
    lj = lax.broadcasted_iota(jnp.int32, (LANES, LANES), 1) // DN_HEAD_DIM
    same_head = (li == lj).astype(F32)
    half = DN_HEAD_DIM // 2

    def l2norm(y):
        return y * lax.rsqrt(_dot_split_lhs(y * y, same_head.astype(BF16)) + NORM_EPS)

    def rope(y):
        lane = lax.broadcasted_iota(jnp.int32, y.shape, 1)
        partner = jnp.where(lane % DN_HEAD_DIM < half,
                            pltpu.roll(y, LANES - half, axis=1), pltpu.roll(y, half, axis=1))
        return y * cos_ref[...] + partner * sin_ref[...]

    off = 0
    for q_ref, k_ref, v_ref, use_rope in ((cq_ref, ck_ref, cv_ref, False), (lq_ref, lk_ref, lv_ref, True)):
        q = l2norm(_conv_silu(q_ref[0], wq_ref[...]))
        k = l2norm(_conv_silu(k_ref[0], wk_ref[...]))
        v = _conv_silu(v_ref[0], wv_ref[...])
        if use_rope:
            q = rope(q)
            k = rope(k)
        q = q * DN_HEAD_DIM ** -0.5
        tx = q.shape[0]
        rows = slice(off, off + tx)
        left = lax.broadcasted_iota(jnp.int32, q.shape, 1) < DN_HEAD_DIM
        k_sw = pltpu.roll(k, DN_HEAD_DIM, axis=1)
        q_sw = pltpu.roll(q, DN_HEAD_DIM, axis=1)
        vk_ref[0, 0, rows, :] = jnp.where(left, v, k_sw)
        vk_ref[0, 1, rows, :] = jnp.where(left, k_sw, v)
        qp_ref[0, 0, rows, :] = jnp.where(left, 0.0, q_sw)
        qp_ref[0, 1, rows, :] = jnp.where(left, q_sw, 0.0)
        kt_ref[0, 0, :, rows] = k_sw.T
        off += tx


def _dn_prepare(z_c, z_l, conv_w, cos, sin):
    b = z_l.shape[0]
    n_hp = DN_PAIRS

    def part(tx, off):
        return pl.BlockSpec((1, tx, LANES), lambda bi, hp: (bi, 0, off + hp))

    def wpart(off):
        return pl.BlockSpec((DN_CONV, LANES), lambda bi, hp: (0, off + hp))

    tbl = pl.BlockSpec((SEQ, LANES), lambda bi, hp: (0, 0))
    per_head = pl.BlockSpec((1, 2, DN_T, LANES), lambda bi, hp: (bi, hp, 0, 0))
    return pl.pallas_call(
        _dnprep_kernel, grid=(b, n_hp),
        in_specs=[part(CTX_LEN, 0), part(CTX_LEN, n_hp), part(CTX_LEN, 2 * n_hp),
                  part(SEQ, 0), part(SEQ, n_hp), part(SEQ, 2 * n_hp),
                  wpart(0), wpart(n_hp), wpart(2 * n_hp), tbl, tbl],
        out_specs=[per_head, per_head, pl.BlockSpec((1, 1, LANES, DN_T), lambda bi, hp: (bi, hp, 0, 0))],
        out_shape=[jax.ShapeDtypeStruct((b, DN_HEADS, DN_T, LANES), F32),
                   jax.ShapeDtypeStruct((b, DN_HEADS, DN_T, LANES), F32),
                   jax.ShapeDtypeStruct((b, n_hp, LANES, DN_T), F32)],
        compiler_params=_params(("parallel", "parallel"), VMEM_LIMIT), name="dn_prepare",
    )(z_c, z_c, z_c, z_l, z_l, z_l, conv_w, conv_w, conv_w, cos, sin)


def _gate_select_matrices():
    sel = np.zeros((DN_PAIRS, LANES, LANES), np.float32)
    for hp in range(DN_PAIRS):
        for d in range(2):
            for hh in range(2):
                src = d * DN_HEADS + 2 * hp + hh
                sel[hp, src, 2 * d + hh] = 1.0
                sel[hp, 2 * DN_HEADS + src, 4 + 2 * d + hh] = 1.0
    return jnp.asarray(sel)


def _softplus(x):
    return jnp.maximum(x, 0.0) + jnp.log1p(jnp.exp(-jnp.abs(x)))


def _gates_kernel(zc_ref, zl_ref, nega_ref, dtb_ref, sel_ref, gc_ref, gr_ref):
    z = jnp.concatenate([zc_ref[0], zl_ref[0]], axis=0)
    lane = lax.broadcasted_iota(jnp.int32, (DN_CHUNK, LANES), 1)
    ri = lax.broadcasted_iota(jnp.int32, (DN_CHUNK, DN_CHUNK), 0)
    ci = lax.broadcasted_iota(jnp.int32, (DN_CHUNK, DN_CHUNK), 1)
    lower = (ri >= ci).astype(F32)
    upper = (ri <= ci).astype(F32)
    beta = jax.nn.sigmoid(z)
    g = nega_ref[...] * _softplus(z + dtb_ref[...])
    tiles = []
    for c in range(DN_NCHUNK):
        rows = slice(c * DN_CHUNK, (c + 1) * DN_CHUNK)
        gc = g[rows]
        prefix = jnp.dot(lower, gc, precision=HIGHEST, preferred_element_type=F32)
        suffix = jnp.dot(upper, gc, precision=HIGHEST, preferred_element_type=F32)
        tiles.append(jnp.where(lane < 2 * DN_HEADS, beta[rows], jnp.where(lane < 3 * DN_HEADS, prefix, suffix)))
    tile = jnp.concatenate(tiles, axis=0)
    for hp in range(DN_PAIRS):
        cols = jnp.dot(tile, sel_ref[hp], precision=HIGHEST, preferred_element_type=F32)
        gc_ref[0, hp] = cols
        gr_ref[0, hp] = cols.T[0:8, :]


def _dn_gates(z_c, z_l, neg_a, dt_bias, sel):
    b = z_l.shape[0]
    return pl.pallas_call(
        _gates_kernel, grid=(b,),
        in_specs=[pl.BlockSpec((1, CTX_LEN, LANES), lambda bi: (bi, 0, 0)),
                  pl.BlockSpec((1, SEQ, LANES), lambda bi: (bi, 0, 0)),
                  pl.BlockSpec((1, LANES), lambda bi: (0, 0)),
                  pl.BlockSpec((1, LANES), lambda bi: (0, 0)),
                  pl.BlockSpec(sel.shape, lambda bi: (0, 0, 0))],
        out_specs=[pl.BlockSpec((1, DN_PAIRS, DN_T, LANES), lambda bi: (bi, 0, 0, 0)),
                   pl.BlockSpec((1, DN_PAIRS, 8, DN_T), lambda bi: (bi, 0, 0, 0))],
        out_shape=[jax.ShapeDtypeStruct((b, DN_PAIRS, DN_T, LANES), F32),
                   jax.ShapeDtypeStruct((b, DN_PAIRS, 8, DN_T), F32)],
        compiler_params=_params(("parallel",), VMEM_LIMIT), name="dn_gates",
    )(z_c, z_l, neg_a, dt_bias, sel)


def _dot_bf16(a, b):
    return jnp.dot(a.astype(BF16), b.astype(BF16), preferred_element_type=F32)


def _block_masks():
    ri = lax.broadcasted_iota(jnp.int32, (LANES, LANES), 0)
    ci = lax.broadcasted_iota(jnp.int32, (LANES, LANES), 1)
    same = (ri // DN_CHUNK) == (ci // DN_CHUNK)
    rin = ri % DN_CHUNK
    cin = ci % DN_CHUNK
    incl = (same & (rin >= cin), same & (rin <= cin))
    strict = (same & (rin > cin), same & (rin < cin))
    return same, incl, strict


def _scan_phase_a(p, vk_ref, qp_ref, kt_ref, gc_ref, gr_ref, lws_ref, lfin_ref, gl_ref):
    c = DN_CHUNK
    same, incl, strict = _block_masks()
    other = jnp.logical_not(same)
    rows = pl.ds(pl.multiple_of(p * LANES, LANES), LANES)
    vk_a = vk_ref[0, 0, rows, :]
    vk_b = vk_ref[0, 1, rows, :]
    qp_a = qp_ref[0, 0, rows, :]
    qp_b = qp_ref[0, 1, rows, :]
    kts = kt_ref[0, 0, :, rows]
    ktr = pltpu.roll(kts, c, axis=1)
    gct = gc_ref[0, 0, rows, :]
    grt = gr_ref[0, 0, :, rows]
    grr = pltpu.roll(grt, c, axis=1)
    lane8 = lax.broadcasted_iota(jnp.int32, (8, LANES), 1)
    left8 = lane8 < c

    items = []
    for par in range(2):
        hs = slice(par * c, (par + 1) * c)
        vks = jnp.concatenate([vk_a[hs], vk_b[hs]], axis=0)
        qq = jnp.concatenate([qp_a[hs], qp_b[hs]], axis=0)
        kk_in = jnp.where(other, vks, 0.0)
        kk = lax.dot_general(jnp.concatenate([kk_in, qq], axis=0).astype(BF16), kk_in.astype(BF16),
                             (((1,), (1,)), ((), ())), preferred_element_type=F32)
        kt_raw = (jnp.concatenate([ktr[:c], kts[c:]], axis=0) if par == 0
                  else jnp.concatenate([kts[:c], ktr[c:]], axis=0))
        for d in range(2):
            bcol = lambda col: jnp.broadcast_to(gct[hs, col:col + 1], (c, LANES))
            beta = jnp.concatenate([bcol(2 * d), bcol(2 * d + 1)], axis=0)
            g_rows = jnp.concatenate([bcol(4 + 2 * d), bcol(5 + 2 * d)], axis=0)
            if par == 0:
                g_lane = jnp.where(left8, grt[4 + 2 * d:5 + 2 * d], grr[5 + 2 * d:6 + 2 * d])
            else:
                g_lane = jnp.where(left8, grr[4 + 2 * d:5 + 2 * d], grt[5 + 2 * d:6 + 2 * d])
            last = par * c + (c - 1 if d == 0 else 0)
            gl_a = jnp.broadcast_to(grt[4 + 2 * d:5 + 2 * d, last:last + 1], (8, LANES))
            gl_b = jnp.broadcast_to(grt[5 + 2 * d:6 + 2 * d, last:last + 1], (8, LANES))
            g_last = jnp.where(left8, gl_a, gl_b)
            g_cols = jnp.broadcast_to(g_lane[0:1], (LANES, LANES))
            decay = jnp.where(incl[d], jnp.exp(jnp.where(incl[d], g_rows - g_cols, 0.0)), 0.0)
            a = jnp.where(strict[d], kk[:LANES] * decay, 0.0) * beta
            qk = kk[LANES:] * decay
            eg = jnp.exp(g_rows)
            x = vks * beta * jnp.where(other, eg, 1.0)
            qe = qq * eg
            kt_s = jnp.where(other, kt_raw * jnp.exp(jnp.broadcast_to(g_last[0:1] - g_lane[0:1], (LANES, LANES))),
                             0.0)
            idx = (2 * p + par) * 2 + d
            lws_ref[idx, LANES:, :] = qe.astype(BF16)
            lfin_ref[idx] = jnp.concatenate([qk, kt_s], axis=0).astype(BF16)
            gl_ref[idx] = jnp.exp(g_last)
            items.append(dict(idx=idx, d=d, a=a, x=x))
    return items


def _scan_solve(items, xs_ref, lws_ref):
    ri = lax.broadcasted_iota(jnp.int32, (LANES, LANES), 0)
    ci = lax.broadcasted_iota(jnp.int32, (LANES, LANES), 1)
    eye = (ri == ci).astype(F32)
    base = DN_INV_BASE

    def off_block(size, d):
        row_second = (ri // size) % 2 == 1
        col_second = (ci // size) % 2 == 1
        inside = (ri // (2 * size)) == (ci // (2 * size))
        return inside & ((row_second & ~col_second) if d == 0 else (~row_second & col_second))

    for it in items:
        it["a8"] = jnp.where((ri // base) == (ci // base), it["a"], 0.0)
        it["t"] = eye - it["a8"]
    for it in items:
        it["p"] = _dot_bf16(it["a8"], it["a8"])
    for it in items:
        r = _dot_bf16(it["p"], jnp.concatenate([it["p"], it["t"]], axis=1))
        it["p"] = r[:, :LANES]
        it["t"] = it["t"] + r[:, LANES:]
    for it in items:
        it["t"] = it["t"] + _dot_bf16(it["p"], it["t"])
    size = base
    while size < DN_CHUNK:
        for it in items:
            it["m"] = _dot_bf16(jnp.where(off_block(size, it["d"]), it["a"], 0.0), it["t"])
        for it in items:
            it["t"] = it["t"] - _dot_bf16(it["t"], it["m"])
        size *= 2
    for it in items:
        it["x"] = _dot_bf16(it["t"], it["x"])
    for it in items:
        xs_ref[it["idx"]] = it["x"]
        lws_ref[it["idx"], :LANES, :] = it["x"].astype(BF16)


def _scan_phase_b(lo, hi, states, xs_ref, lws_ref, lfin_ref, gl_ref, out_refs):
    c = DN_CHUNK
    same, _, _ = _block_masks()
    other = jnp.logical_not(same)
    left = lax.broadcasted_iota(jnp.int32, (c, LANES), 1) < c

    def body(s, carry):
        u_f = s
        u_b = jnp.where(s < DN_CTX_CHUNKS, DN_CTX_CHUNKS - 1 - s, DN_NCHUNK + DN_CTX_CHUNKS - 1 - s)
        us = (u_f, u_b)
        idx = [us[d] * 2 + d for d in range(2)]
        res = [jnp.dot(lws_ref[idx[d]], carry[d].astype(BF16), preferred_element_type=F32) for d in range(2)]
        vn = [(xs_ref[idx[d]] - res[d][:LANES]).astype(BF16) for d in range(2)]
        fin = [jnp.dot(lfin_ref[idx[d]], vn[d], preferred_element_type=F32) for d in range(2)]
        new = []
        for d in range(2):
            if out_refs is not None:
                o = jnp.where(left, fin[d][:c] + res[d][LANES:LANES + c],
                              fin[d][c:LANES] + res[d][LANES + c:])
                out_refs[d][0, pl.ds(pl.multiple_of((us[d] - DN_CTX_CHUNKS) * c, c), c), :] = o
            decay = jnp.broadcast_to(gl_ref[idx[d]][0:1], (LANES, LANES))
            new.append(carry[d] * decay + jnp.where(other, fin[d][LANES:], 0.0))
        return tuple(new)

    return lax.fori_loop(lo, hi, body, states)


def _scan_kernel(vk_ref, qp_ref, kt_ref, gc_ref, gr_ref, of_ref, ob_ref, xs_ref, lws_ref, lfin_ref, gl_ref):
    def phase_a(step, carry):
        items = []
        for j in range(DN_PAIRS_PER_STEP):
            items += _scan_phase_a(step * DN_PAIRS_PER_STEP + j, vk_ref, qp_ref, kt_ref, gc_ref, gr_ref,
                                   lws_ref, lfin_ref, gl_ref)
        _scan_solve(items, xs_ref, lws_ref)
        return carry

    lax.fori_loop(0, DN_T // LANES // DN_PAIRS_PER_STEP, phase_a, 0)
    zero = jnp.zeros((LANES, LANES), F32)
    scratch = (xs_ref, lws_ref, lfin_ref, gl_ref)
    st = _scan_phase_b(0, DN_CTX_CHUNKS, (zero, zero), *scratch, None)
    _scan_phase_b(DN_CTX_CHUNKS, DN_NCHUNK, st, *scratch, (of_ref, ob_ref))


def _dn_scan(vk, qp, kt, g_cols, g_rows):
    b = vk.shape[0]
    n_items = 2 * DN_NCHUNK
    out = pl.BlockSpec((1, SEQ, LANES), lambda bi, hp: (bi, 0, hp))
    return pl.pallas_call(
        _scan_kernel, grid=(b, DN_PAIRS),
        in_specs=[pl.BlockSpec((1, 2, DN_T, LANES), lambda bi, hp: (bi, hp, 0, 0)),
                  pl.BlockSpec((1, 2, DN_T, LANES), lambda bi, hp: (bi, hp, 0, 0)),
                  pl.BlockSpec((1, 1, LANES, DN_T), lambda bi, hp: (bi, hp, 0, 0)),
                  pl.BlockSpec((1, 1, DN_T, LANES), lambda bi, hp: (bi, hp, 0, 0)),
                  pl.BlockSpec((1, 1, 8, DN_T), lambda bi, hp: (bi, hp, 0, 0))],
        out_specs=[out, out],
        out_shape=[jax.ShapeDtypeStruct((b, SEQ, DN_WIDTH), F32)] * 2,
        scratch_shapes=[pltpu.VMEM((n_items, LANES, LANES), F32),
                        pltpu.VMEM((n_items, 2 * LANES, LANES), BF16),
                        pltpu.VMEM((n_items, 2 * LANES, LANES), BF16),
                        pltpu.VMEM((n_items, 8, LANES), F32)],
        compiler_params=_params(("parallel", "parallel"), VMEM_LIMIT), name="dn_scan",
    )(vk, qp, kt, g_cols, g_rows)


POST_TM = 256
ROW_TILE = 8


def _store_row_tiles(ref, val):
    m = val.shape[0]
    for j in range(ROW_TILE):
        ref[pl.ds(j, m, stride=ROW_TILE), :] = val[:, j * LANES:(j + 1) * LANES]


def _load_row_tiles(ref, first, m):
    return jnp.concatenate([ref[pl.ds(first * ROW_TILE + j, m, stride=ROW_TILE), :] for j in range(ROW_TILE)],
                           axis=1)


def _post_kernel(of_ref, ob_ref, dg_ref, ona_ref, gab_ref, x_ref, g1_ref, sh2_ref, sc2_ref,
                 dnw_ref, wa_ref, wb_ref, wo_ref, n2w_ref, wr_ref, br_ref,
                 xl_ref, h2_ref, route_ref, cnt_ref):
    i = pl.program_id(0)

    @pl.when(i == 0)
    def _():
        cnt_ref[...] = jnp.zeros_like(cnt_ref)

    o = of_ref[...] + ob_ref[...]
    hi = lax.broadcasted_iota(jnp.int32, (DN_WIDTH, DN_WIDTH), 0) // DN_HEAD_DIM
    hj = lax.broadcasted_iota(jnp.int32, (DN_WIDTH, DN_WIDTH), 1) // DN_HEAD_DIM
    head_mean = jnp.where(hi == hj, 1.0 / DN_HEAD_DIM, 0.0).astype(BF16)
    ms = _dot_split_lhs(o * o, head_mean)
    dg = dg_ref[...]
    o_dn = (o * lax.rsqrt(ms + NORM_EPS) * dnw_ref[...]) * (dg * jax.nn.sigmoid(dg))
    gab = gab_ref[...]
    ya = jnp.dot(ona_ref[...], wa_ref[...], preferred_element_type=F32)
    yb = jnp.dot(o_dn.astype(BF16), wb_ref[...], preferred_element_type=F32)
    y = jax.nn.sigmoid(gab[:, :D_MODEL]) * ya + jax.nn.sigmoid(gab[:, D_MODEL:]) * yb
    y = jnp.dot(y.astype(BF16), wo_ref[...], preferred_element_type=F32)
    xl = x_ref[...] + g1_ref[0] * y
    xl_ref[...] = xl
    ms2 = jnp.mean(xl * xl, axis=-1, keepdims=True)
    h2 = (xl * lax.rsqrt(ms2 + NORM_EPS) * n2w_ref[...]) * (1.0 + sc2_ref[0]) + sh2_ref[0]
    _store_row_tiles(h2_ref, h2)

    h_hi, h_lo = _split_bf16(h2)
    logits = (jnp.dot(h_hi, wr_ref[0], preferred_element_type=F32) + jnp.dot(h_lo, wr_ref[0], preferred_element_type=F32)
              + jnp.dot(h_hi, wr_ref[1], preferred_element_type=F32) + br_ref[...])
    tm = logits.shape[0]
    lane = lax.broadcasted_iota(jnp.int32, (tm, LANES), 1).astype(F32)
    vals, idxs = [], []
    cur = logits
    for _ in range(TOP_K):
        m = jnp.max(cur, axis=-1, keepdims=True)
        idx = jnp.min(jnp.where(cur == m, lane, float(LANES)), axis=-1, keepdims=True)
        vals.append(m)
        idxs.append(idx)
        cur = jnp.where(lane == idx, -jnp.inf, cur)
    es = [jnp.exp(v - vals[0]) for v in vals]
    den = es[0] + es[1] + es[2] + es[3]
    onehot = jnp.zeros((tm, LANES), F32)
    for idx in idxs:
        onehot = onehot + jnp.where(lane == idx, 1.0, 0.0)
    ti = lax.broadcasted_iota(jnp.int32, (tm, tm), 0)
    tj = lax.broadcasted_iota(jnp.int32, (tm, tm), 1)
    before = jnp.where(ti > tj, 1.0, 0.0).astype(BF16)
    cnt = cnt_ref[...] + jnp.dot(before, onehot.astype(BF16), preferred_element_type=F32)
    route = jnp.zeros((tm, LANES), F32)
    for kk in range(TOP_K):
        rank = jnp.sum(jnp.where(lane == idxs[kk], cnt, 0.0), axis=-1, keepdims=True)
        route = jnp.where(lane == float(kk), es[kk] / den, route)
        route = jnp.where(lane == float(TOP_K + kk), idxs[kk], route)
        route = jnp.where(lane == float(2 * TOP_K + kk), rank, route)
    route_ref[...] = route
    cnt_ref[...] = cnt_ref[...] + jnp.sum(onehot, axis=0, keepdims=True)


def _post_mixer(o_f, o_b, dg, o_na, gab, x, g1, sh2, sc2, dn_norm_w, w_br_a, w_br_b, w_out, norm2_w, wr, br):
    n, d = x.shape
    tm = POST_TM
    per_batch = SEQ // tm
    tok = lambda w: pl.BlockSpec((tm, w), lambda i: (i, 0))
    row = pl.BlockSpec((1, 1, d), lambda i: (i // per_batch, 0, 0))
    const = lambda a: pl.BlockSpec(a.shape, lambda i: (0,) * a.ndim)
    return pl.pallas_call(
        _post_kernel, grid=(n // tm,),
        in_specs=[tok(DN_WIDTH), tok(DN_WIDTH), tok(DN_WIDTH), tok(NA_WIDTH), tok(2 * D_MODEL), tok(d),
                  row, row, row, const(dn_norm_w), const(w_br_a), const(w_br_b), const(w_out),
                  const(norm2_w), const(wr), const(br)],
        out_specs=[tok(d), pl.BlockSpec((tm * ROW_TILE, LANES), lambda i: (i, 0)), tok(LANES),
                   pl.BlockSpec((1, LANES), lambda i: (0, 0))],
        out_shape=[jax.ShapeDtypeStruct((n, d), F32), jax.ShapeDtypeStruct((n * ROW_TILE, LANES), F32),
                   jax.ShapeDtypeStruct((n, LANES), F32), jax.ShapeDtypeStruct((1, LANES), F32)],
        compiler_params=_params(("arbitrary",), VMEM_LIMIT), name="post_mixer_router",
    )(o_f, o_b, dg, o_na, gab, x, g1, sh2, sc2, dn_norm_w, w_br_a, w_br_b, w_out, norm2_w, wr, br)


MOE_LOOKAHEAD = 2
MOE_SLOTS = MOE_LOOKAHEAD + 1
DMA_ISSUE_UNROLL = 8
DMA_WAIT_UNROLL = 32


def _row_copy(src_hbm, dst_vmem, src_row, dst_row, sem):
    src = pl.ds(pl.multiple_of(src_row * ROW_TILE, ROW_TILE), ROW_TILE)
    dst = pl.ds(pl.multiple_of(dst_row * ROW_TILE, ROW_TILE), ROW_TILE)
    return pltpu.make_async_copy(src_hbm.at[src], dst_vmem.at[dst], sem)


def _expert_kernel(blk_e_ref, nact_ref, tok_ref, h_hbm, w1_ref, b1_ref, w2_ref, b2_ref, y_ref,
                   xbuf, w1b, w2b, sems):
    i = pl.program_id(0)
    n_active = nact_ref[0]
    bm = MOE_BM

    def issue(blk, slot):
        def body(g, carry):
            for j in range(DMA_ISSUE_UNROLL):
                s = g * DMA_ISSUE_UNROLL + j
                _row_copy(h_hbm, xbuf.at[slot], tok_ref[blk * bm + s], s, sems.at[slot]).start()
            return carry
        lax.fori_loop(0, bm // DMA_ISSUE_UNROLL, body, 0)

    def wait(slot):
        def body(g, carry):
            for j in range(DMA_WAIT_UNROLL):
                _row_copy(h_hbm, xbuf.at[slot], 0, g * DMA_WAIT_UNROLL + j, sems.at[slot]).wait()
            return carry
        lax.fori_loop(0, bm // DMA_WAIT_UNROLL, body, 0)

    @pl.when(i == 0)
    def _():
        for j in range(MOE_LOOKAHEAD):
            issue(jnp.minimum(j, n_active - 1), j)

    changed = jnp.logical_or(i == 0, blk_e_ref[i] != blk_e_ref[jnp.maximum(i - 1, 0)])

    @pl.when(jnp.logical_and(i < n_active, changed))
    def _():
        w1b[...] = w1_ref[0].astype(BF16)
        w2b[...] = w2_ref[0].astype(BF16)

    @pl.when(i < n_active)
    def _():
        slot = i % MOE_SLOTS
        wait(slot)
        xb = _load_row_tiles(xbuf.at[slot], 0, bm).astype(BF16)
        nxt = jnp.minimum(i + MOE_LOOKAHEAD, n_active - 1)
        nslot = (i + MOE_LOOKAHEAD) % MOE_SLOTS
        for s in range(bm):
            _row_copy(h_hbm, xbuf.at[nslot], tok_ref[nxt * bm + s], s, sems.at[nslot]).start()
        hb = jnp.dot(xb, w1b[...], preferred_element_type=F32) + b1_ref[0]
        gate = jnp.minimum(hb[:, :D_EXPERT], SWIGLU_LIMIT)
        up = jnp.clip(hb[:, D_EXPERT:], -SWIGLU_LIMIT, SWIGLU_LIMIT)
        act = (up + 1.0) * gate * jax.nn.sigmoid(SWIGLU_ALPHA * gate)
        _store_row_tiles(y_ref, jnp.dot(act.astype(BF16), w2b[...], preferred_element_type=F32) + b2_ref[0])

    @pl.when(i == n_active)
    def _():
        for j in range(MOE_LOOKAHEAD):
            wait((i + j) % MOE_SLOTS)

    @pl.when(i >= n_active)
    def _():
        y_ref[...] = jnp.zeros_like(y_ref)


def _expert_ffn(blk_expert, n_active, tok_pad, h2, w1, b1, w2, b2):
    d = w1.shape[1]
    f2 = w1.shape[2]

    def live(i, nact):
        return jnp.minimum(i, jnp.maximum(nact[0] - 1, 0))

    grid_spec = pltpu.PrefetchScalarGridSpec(
        num_scalar_prefetch=3, grid=(MOE_NBLK,),
        in_specs=[pl.BlockSpec(memory_space=pl.ANY),
                  pl.BlockSpec((1, d, f2), lambda i, be, na, tk: (be[live(i, na)], 0, 0)),
                  pl.BlockSpec((1, 1, f2), lambda i, be, na, tk: (be[live(i, na)], 0, 0)),
                  pl.BlockSpec((1, f2 // 2, d), lambda i, be, na, tk: (be[live(i, na)], 0, 0)),
                  pl.BlockSpec((1, 1, d), lambda i, be, na, tk: (be[live(i, na)], 0, 0))],
        out_specs=pl.BlockSpec((MOE_BM * ROW_TILE, LANES), lambda i, be, na, tk: (i, 0)),
        scratch_shapes=[pltpu.VMEM((MOE_SLOTS, MOE_BM * ROW_TILE, LANES), F32), pltpu.VMEM((d, f2), BF16),
                        pltpu.VMEM((f2 // 2, d), BF16), pltpu.SemaphoreType.DMA((MOE_SLOTS,))])
    return pl.pallas_call(
        _expert_kernel, grid_spec=grid_spec,
        out_shape=jax.ShapeDtypeStruct((MOE_MPAD * ROW_TILE, LANES), F32),
        compiler_params=_params(("arbitrary",), VMEM_LIMIT), name="expert_ffn",
    )(blk_expert, n_active, tok_pad, h2, w1, b1, w2, b2)


COMB_TM = 128


def _combine_kernel(dest_ref, y_hbm, xl_ref, route_ref, g2_ref, fw_ref, o_ref, ybuf, sems):
    i = pl.program_id(0)
    n_steps = pl.num_programs(0)
    tm = COMB_TM

    def issue(step, slot):
        def body(g, carry):
            for j in range(DMA_ISSUE_UNROLL // TOP_K):
                t = g * (DMA_ISSUE_UNROLL // TOP_K) + j
                for kk in range(TOP_K):
                    _row_copy(y_hbm, ybuf.at[slot], dest_ref[(step * tm + t) * TOP_K + kk], kk * tm + t,
                              sems.at[slot]).start()
            return carry
        lax.fori_loop(0, tm * TOP_K // DMA_ISSUE_UNROLL, body, 0)

    def wait(slot):
        def body(g, carry):
            for j in range(DMA_WAIT_UNROLL):
                _row_copy(y_hbm, ybuf.at[slot], 0, g * DMA_WAIT_UNROLL + j, sems.at[slot]).wait()
            return carry
        lax.fori_loop(0, tm * TOP_K // DMA_WAIT_UNROLL, body, 0)

    @pl.when(i == 0)
    def _():
        issue(0, 0)

    @pl.when(i + 1 < n_steps)
    def _():
        issue(i + 1, (i + 1) % 2)

    slot = i % 2
    wait(slot)
    route = route_ref[...]
    moe = jnp.zeros((tm, D_MODEL), F32)
    for kk in range(TOP_K):
        moe = moe + route[:, kk:kk + 1] * _load_row_tiles(ybuf.at[slot], kk * tm, tm)
    xo = xl_ref[...] + g2_ref[0] * moe
    ms = jnp.mean(xo * xo, axis=-1, keepdims=True)
    o_ref[...] = xo * lax.rsqrt(ms + NORM_EPS) * fw_ref[...]


def _combine(dest, y_sorted, xl, route, g2, final_w):
    n, d = xl.shape
    tm = COMB_TM
    per_batch = SEQ // tm
    grid_spec = pltpu.PrefetchScalarGridSpec(
        num_scalar_prefetch=1, grid=(n // tm,),
        in_specs=[pl.BlockSpec(memory_space=pl.ANY),
                  pl.BlockSpec((tm, d), lambda i, ds: (i, 0)),
                  pl.BlockSpec((tm, LANES), lambda i, ds: (i, 0)),
                  pl.BlockSpec((1, 1, d), lambda i, ds: (i // per_batch, 0, 0)),
                  pl.BlockSpec((1, d), lambda i, ds: (0, 0))],
        out_specs=pl.BlockSpec((tm, d), lambda i, ds: (i, 0)),
        scratch_shapes=[pltpu.VMEM((2, TOP_K * tm * ROW_TILE, LANES), F32), pltpu.SemaphoreType.DMA((2,))])
    return pl.pallas_call(
        _combine_kernel, grid_spec=grid_spec,
        out_shape=jax.ShapeDtypeStruct((n, d), F32),
        compiler_params=_params(("arbitrary",), VMEM_LIMIT), name="moe_combine",
    )(dest, y_sorted, xl, route, g2, final_w)


def _split_in_weights(w_in):
    o = np.cumsum((0, NA_WIDTH, NA_WIDTH, NA_WIDTH, 3 * DN_WIDTH, DN_WIDTH, 2 * DN_HEADS, 2 * DN_HEADS,
                   D_MODEL, D_MODEL))
    wb = w_in.astype(BF16)
    w_na = wb[:, o[0]:o[3]]
    w_dq = wb[:, o[3]:o[4]]
    w_dg = wb[:, o[4]:o[5]]
    w_gab = wb[:, o[7]:o[9]]
    w_dba = jnp.pad(wb[:, o[5]:o[7]], ((0, 0), (0, LANES - 4 * DN_HEADS)))
    return w_na, w_dq, w_dg, w_gab, w_dba


def kernel(x, c, ctx, c_ctx, w_mod, b_mod, norm1_w, w_in, na_rpb, dn_conv_w, dn_a_log, dn_dt_bias, dn_norm_w,
           w_br_a, w_br_b, w_out, norm2_w, w_router, b_router, w1, b1, w2, b2, final_norm_w):
    d = D_MODEL
    cc = jnp.concatenate([c, c_ctx[None], jnp.zeros((16 - BATCH - 1, d), F32)], axis=0)
    mod = _modulation(cc, w_mod[0], b_mod[0][None])
    mod_l = mod[:BATCH].reshape(BATCH, 6, 1, d)
    sh1, sc1, g1, sh2, sc2, g2 = (mod_l[:, i] for i in range(6))
    mod_c = jnp.broadcast_to(mod[BATCH].reshape(6, 1, 1, d), (6, BATCH, 1, d))

    weights = _split_in_weights(w_in[0])
    n1w = norm1_w[0][None]
    naq, nak, nav, dqkv_l, dg_l, gab_l, dba_l = _in_projection(x, sh1, sc1, n1w, weights, 512)
    _, nak_c, nav_c, dqkv_c, _, _, dba_c = _in_projection(ctx, mod_c[0], mod_c[1], n1w, weights, CTX_LEN)

    o_na = _neighborhood_attention(naq, nak, nav, nak_c, nav_c, _na_bias_table(na_rpb[0]))

    cos, sin = _rope_tables(SEQ)
    vk, qp, kt = _dn_prepare(dqkv_c, dqkv_l, dn_conv_w[0], cos, sin)
    pad16 = lambda v: jnp.pad(v.reshape(1, 2 * DN_HEADS), ((0, 0), (2 * DN_HEADS, LANES - 4 * DN_HEADS)))
    g_cols, g_rows = _dn_gates(dba_c, dba_l, pad16(-jnp.exp(dn_a_log[0])), pad16(dn_dt_bias[0]),
                               _gate_select_matrices())
    o_f, o_b = _dn_scan(vk, qp, kt, g_cols, g_rows)

    flat = lambda a: a.reshape(N_TOK, a.shape[-1])
    wr = jnp.stack(_split_bf16(jnp.pad(w_router[0], ((0, 0), (0, LANES - N_EXPERTS)))))
    br = jnp.concatenate([b_router[0], jnp.full((LANES - N_EXPERTS,), NEG_INF, F32)])[None]
    dnw = jnp.tile(dn_norm_w[0], DN_HEADS)[None]
    xl, h2, route, counts = _post_mixer(
        flat(o_f), flat(o_b), flat(dg_l), flat(o_na), flat(gab_l), flat(x), g1, sh2, sc2, dnw,
        w_br_a[0].astype(BF16), w_br_b[0].astype(BF16), w_out[0].astype(BF16), norm2_w[0][None], wr, br)

    e_idx = route[:, TOP_K:2 * TOP_K].astype(jnp.int32)
    rank = route[:, 2 * TOP_K:3 * TOP_K].astype(jnp.int32)
    cnt = counts[0, :N_EXPERTS].astype(jnp.int32)
    padded = (cnt + MOE_BM - 1) // MOE_BM * MOE_BM
    pad_ends = jnp.cumsum(padded)
    pad_starts = pad_ends - padded
    dest = (pad_starts[e_idx] + rank).reshape(-1)
    tok_pad = jnp.zeros((MOE_MPAD,), jnp.int32).at[dest].set(jnp.arange(N_ASG, dtype=jnp.int32) // TOP_K)
    blk_first_row = jnp.arange(MOE_NBLK, dtype=jnp.int32) * MOE_BM
    blk_expert = jnp.minimum(jnp.sum((pad_ends[None, :] <= blk_first_row[:, None]).astype(jnp.int32), axis=1),
                             N_EXPERTS - 1)
    n_active = (pad_ends[-1:] // MOE_BM).astype(jnp.int32)

    y_sorted = _expert_ffn(blk_expert, n_active, tok_pad, h2, w1[0], b1[0][:, None], w2[0], b2[0][:, None])
    out = _combine(dest, y_sorted, xl, route, g2, final_norm_w[None])
    return out.reshape(BATCH, SEQ, d)
```

```python
import functools

import numpy as np
import jax
import jax.numpy as jnp
from jax import lax
from jax.experimental import pallas as pl
from jax.experimental.pallas import tpu as pltpu

F32 = jnp.float32
BF16 = jnp.bfloat16
HIGHEST = lax.Precision.HIGHEST

D_MODEL = 1024
BATCH = 8
SEQ = 2048
GRID_W = 64
GRID_ROWS = SEQ // GRID_W
CTX_LEN = 256
NA_HEADS = 8
NA_HEAD_DIM = 64
NA_KH = 8
NA_KW = 16
NA_SCALE = NA_HEAD_DIM ** -0.5
NA_WIDTH = NA_HEADS * NA_HEAD_DIM
DN_HEADS = 8
DN_HEAD_DIM = 64
DN_WIDTH = DN_HEADS * DN_HEAD_DIM
DN_CONV = 5
DN_CHUNK = 64
ROPE_THETA = 10000.0
N_EXPERTS = 32
TOP_K = 4
D_EXPERT = 1024
SWIGLU_LIMIT = 7.0
SWIGLU_ALPHA = 1.702
NORM_EPS = 1e-6
NEG_INF = -1e30

LANES = 128
N_TOK = BATCH * SEQ
N_ASG = N_TOK * TOP_K
MOE_BM = 256
MOE_MPAD = -(-(N_ASG + N_EXPERTS * (MOE_BM - 1)) // MOE_BM) * MOE_BM
MOE_NBLK = MOE_MPAD // MOE_BM
VMEM_LIMIT = 56 * 1024 * 1024

DN_T = CTX_LEN + SEQ
DN_NCHUNK = DN_T // DN_CHUNK
DN_CTX_CHUNKS = CTX_LEN // DN_CHUNK
DN_PAIRS = DN_HEADS // 2
DN_PAIRS_PER_STEP = 6
DN_INV_BASE = 8


def _params(sem, vmem=None):
    return pltpu.CompilerParams(dimension_semantics=sem, vmem_limit_bytes=vmem)


def _split_bf16(x):
    hi = x.astype(BF16)
    return hi, (x - hi.astype(F32)).astype(BF16)


def _dot_split_lhs(x, w_bf16):
    hi, lo = _split_bf16(x)
    return (jnp.dot(hi, w_bf16, preferred_element_type=F32) + jnp.dot(lo, w_bf16, preferred_element_type=F32))


def _mod_kernel(c_ref, w_ref, b_ref, o_ref):
    c = c_ref[...]
    s = c * jax.nn.sigmoid(c)
    o_ref[...] = jnp.dot(s, w_ref[...], precision=HIGHEST, preferred_element_type=F32) + b_ref[...]


def _modulation(cc, w_mod, b_mod):
    rows, d = cc.shape
    n = w_mod.shape[1]
    bn = 1024
    return pl.pallas_call(
        _mod_kernel, grid=(n // bn,),
        in_specs=[pl.BlockSpec((rows, d), lambda j: (0, 0)),
                  pl.BlockSpec((d, bn), lambda j: (0, j)),
                  pl.BlockSpec((1, bn), lambda j: (0, j))],
        out_specs=pl.BlockSpec((rows, bn), lambda j: (0, j)),
        out_shape=jax.ShapeDtypeStruct((rows, n), F32),
        compiler_params=_params(("parallel",)), name="modulation")(cc, w_mod, b_mod)


def _inproj_kernel(x_ref, sh_ref, sc_ref, nw_ref, wna_ref, wdq_ref, wdg_ref, wgab_ref, wdba_ref,
                   q_ref, k_ref, v_ref, dq_ref, dg_ref, gab_ref, dba_ref):
    x = x_ref[0]
    ms = jnp.mean(x * x, axis=-1, keepdims=True)
    y = x * lax.rsqrt(ms + NORM_EPS) * nw_ref[...]
    h = (y * (1.0 + sc_ref[0]) + sh_ref[0]).astype(BF16)
    na = jnp.dot(h, wna_ref[...], preferred_element_type=F32)
    q_ref[0] = (na[:, :NA_WIDTH] * NA_SCALE).astype(BF16)
    k_ref[0] = na[:, NA_WIDTH:2 * NA_WIDTH].astype(BF16)
    v_ref[0] = na[:, 2 * NA_WIDTH:].astype(BF16)
    dq_ref[0] = jnp.dot(h, wdq_ref[...], preferred_element_type=F32)
    dg_ref[0] = jnp.dot(h, wdg_ref[...], preferred_element_type=F32)
    gab_ref[0] = jnp.dot(h, wgab_ref[...], preferred_element_type=F32)
    dba_ref[0] = jnp.dot(h, wdba_ref[...], preferred_element_type=F32)


def _in_projection(x, shift, scale, norm_w, weights, tm):
    bx, tx, d = x.shape
    wna, wdq, wdg, wgab, wdba = weights
    tok = lambda w: pl.BlockSpec((1, tm, w), lambda b, i: (b, i, 0))
    row = pl.BlockSpec((1, 1, d), lambda b, i: (b, 0, 0))
    const = lambda a: pl.BlockSpec(a.shape, lambda b, i: (0,) * a.ndim)
    widths = (NA_WIDTH, NA_WIDTH, NA_WIDTH, 3 * DN_WIDTH, DN_WIDTH, 2 * D_MODEL, LANES)
    dtypes = (BF16, BF16, BF16, F32, F32, F32, F32)
    return pl.pallas_call(
        _inproj_kernel, grid=(bx, tx // tm),
        in_specs=[tok(d), row, row, const(norm_w), const(wna), const(wdq), const(wdg), const(wgab), const(wdba)],
        out_specs=[tok(w) for w in widths],
        out_shape=[jax.ShapeDtypeStruct((bx, tx, w), dt) for w, dt in zip(widths, dtypes)],
        compiler_params=_params(("parallel", "parallel"), VMEM_LIMIT), name="in_projection",
    )(x, shift, scale, norm_w, wna, wdq, wdg, wgab, wdba)


def _na_bias_table(rpb):
    col = np.arange(GRID_W)
    col_start = np.clip(col - NA_KW // 2, 0, GRID_W - NA_KW)
    mask = (col[None, :] >= col_start[:, None]) & (col[None, :] < col_start[:, None] + NA_KW)
    dc = np.clip(col[None, :] - col[:, None] + NA_KW - 1, 0, 2 * NA_KW - 2)
    pick = np.zeros((2 * NA_KW - 1, GRID_W * GRID_W), np.float32)
    pick[dc.reshape(-1), np.arange(GRID_W * GRID_W)] = 1.0
    n_dr = 2 * NA_KH - 1
    by_dr = jnp.dot(rpb.astype(F32).reshape(NA_HEADS * n_dr, 2 * NA_KW - 1), pick, precision=HIGHEST)
    by_dr = jnp.where(mask[None, None], by_dr.reshape(NA_HEADS, n_dr, GRID_W, GRID_W), NEG_INF)
    tbl = jnp.stack([by_dr[:, c:c + NA_KH] for c in range(NA_KH)], axis=0)
    return tbl.transpose(0, 1, 3, 2, 4).reshape(NA_KH, NA_HEADS, GRID_W, NA_KH * GRID_W)


NA_HEAD_GROUP = 4


def _na_first_row(r):
    return jnp.clip(r - NA_KH // 2, 0, GRID_ROWS - NA_KH)


def _na_kernel(q_ref, k_ref, v_ref, kc_ref, vc_ref, bias_ref, o_ref):
    r = pl.program_id(1)
    start = pl.multiple_of(_na_first_row(r) * GRID_W, GRID_W)
    q = q_ref[0]
    kw = k_ref[0, pl.ds(start, NA_KH * GRID_W), :]
    vw = v_ref[0, pl.ds(start, NA_KH * GRID_W), :]
    kc = kc_ref[0]
    vc = vc_ref[0]
    nt = (((1,), (1,)), ((), ()))
    outs = []
    for g0 in range(0, NA_HEADS, NA_HEAD_GROUP):
        sls = [slice(h * NA_HEAD_DIM, (h + 1) * NA_HEAD_DIM) for h in range(g0, g0 + NA_HEAD_GROUP)]
        s = [lax.dot_general(q[:, sl], kw[:, sl], nt, preferred_element_type=F32) + bias_ref[0, g0 + i]
             for i, sl in enumerate(sls)]
        sc = [lax.dot_general(q[:, sl], kc[:, sl], nt, preferred_element_type=F32) for sl in sls]
        m = [jnp.maximum(jnp.max(a, axis=-1, keepdims=True), jnp.max(b, axis=-1, keepdims=True))
             for a, b in zip(s, sc)]
        p = [jnp.exp(a - mm) for a, mm in zip(s, m)]
        pc = [jnp.exp(b - mm) for b, mm in zip(sc, m)]
        denom = [jnp.sum(a, axis=-1, keepdims=True) + jnp.sum(b, axis=-1, keepdims=True) for a, b in zip(p, pc)]
        o = [jnp.dot(a.astype(BF16), vw[:, sl], preferred_element_type=F32)
             + jnp.dot(b.astype(BF16), vc[:, sl], preferred_element_type=F32) for a, b, sl in zip(p, pc, sls)]
        outs += [oo / dd for oo, dd in zip(o, denom)]
    o_ref[0] = jnp.concatenate(outs, axis=1).astype(BF16)


def _neighborhood_attention(q, k, v, kc, vc, bias):
    b, t, w = q.shape

    def bias_map(bi, r):
        return (_na_first_row(r) - r + NA_KH - 1, 0, 0, 0)

    return pl.pallas_call(
        _na_kernel, grid=(b, GRID_ROWS),
        in_specs=[pl.BlockSpec((1, GRID_W, w), lambda bi, r: (bi, r, 0)),
                  pl.BlockSpec((1, t, w), lambda bi, r: (bi, 0, 0)),
                  pl.BlockSpec((1, t, w), lambda bi, r: (bi, 0, 0)),
                  pl.BlockSpec((1, CTX_LEN, w), lambda bi, r: (bi, 0, 0)),
                  pl.BlockSpec((1, CTX_LEN, w), lambda bi, r: (bi, 0, 0)),
                  pl.BlockSpec((1, NA_HEADS, GRID_W, NA_KH * GRID_W), bias_map)],
        out_specs=pl.BlockSpec((1, GRID_W, w), lambda bi, r: (bi, r, 0)),
        out_shape=jax.ShapeDtypeStruct((b, t, w), BF16),
        compiler_params=_params(("parallel", "parallel"), VMEM_LIMIT), name="neighborhood_attention",
    )(q, k, v, kc, vc, bias)


def _rope_tables(t_len):
    t = jnp.arange(t_len)
    row = (t // GRID_W).astype(F32)
    col = (t % GRID_W).astype(F32)
    n_axis = DN_HEAD_DIM // 4
    freqs = ROPE_THETA ** (-jnp.arange(n_axis, dtype=F32) / n_axis)
    ang = jnp.concatenate([row[:, None] * freqs, col[:, None] * freqs], axis=-1)
    cos = jnp.tile(jnp.cos(ang), (1, 4))
    sin = jnp.sin(ang)
    sin_signed = jnp.tile(jnp.concatenate([-sin, sin], axis=-1), (1, 2))
    return cos, sin_signed


def _conv_silu(x, w):
    tx = x.shape[0]
    t = lax.broadcasted_iota(jnp.int32, x.shape, 0)
    acc = x * w[DN_CONV // 2:DN_CONV // 2 + 1, :]
    for tap in range(DN_CONV):
        shift = DN_CONV // 2 - tap
        if shift == 0:
            continue
        xs = pltpu.roll(x, shift % tx, axis=0)
        valid = (t >= shift) & (t < tx + shift)
        acc = acc + jnp.where(valid, xs, 0.0) * w[tap:tap + 1, :]
    return acc * jax.nn.sigmoid(acc)


def _dnprep_kernel(cq_ref, ck_ref, cv_ref, lq_ref, lk_ref, lv_ref, wq_ref, wk_ref, wv_ref, cos_ref, sin_ref,
                   vk_ref, qp_ref, kt_ref):
    li = lax.broadcasted_iota(jnp.int32, (LANES, LANES), 0) // DN_HEAD_DIM
    lj = lax.broadcasted_iota(jnp.int32, (LANES, LANES), 1) // DN_HEAD_DIM
    same_head = (li == lj).astype(F32)
    half = DN_HEAD_DIM // 2

    def l2norm(y):
        return y * lax.rsqrt(_dot_split_lhs(y * y, same_head.astype(BF16)) + NORM_EPS)

    def rope(y):
        lane = lax.broadcasted_iota(jnp.int32, y.shape, 1)
        partner = jnp.where(lane % DN_HEAD_DIM < half,
                            pltpu.roll(y, LANES - half, axis=1), pltpu.roll(y, half, axis=1))
        return y * cos_ref[...] + partner * sin_ref[...]

    off = 0
    for q_ref, k_ref, v_ref, use_rope in ((cq_ref, ck_ref, cv_ref, False), (lq_ref, lk_ref, lv_ref, True)):
        q = l2norm(_conv_silu(q_ref[0], wq_ref[...]))
        k = l2norm(_conv_silu(k_ref[0], wk_ref[...]))
        v = _conv_silu(v_ref[0], wv_ref[...])
        if use_rope:
            q = rope(q)
            k = rope(k)
        q = q * DN_HEAD_DIM ** -0.5
        tx = q.shape[0]
        rows = slice(off, off + tx)
        left = lax.broadcasted_iota(jnp.int32, q.shape, 1) < DN_HEAD_DIM
        k_sw = pltpu.roll(k, DN_HEAD_DIM, axis=1)
        q_sw = pltpu.roll(q, DN_HEAD_DIM, axis=1)
        vk_ref[0, 0, rows, :] = jnp.where(left, v, k_sw)
        vk_ref[0, 1, rows, :] = jnp.where(left, k_sw, v)
        qp_ref[0, 0, rows, :] = jnp.where(left, 0.0, q_sw)
        qp_ref[0, 1, rows, :] = jnp.where(left, q_sw, 0.0)
        kt_ref[0, 0, :, rows] = k_sw.T
        off += tx


def _dn_prepare(z_c, z_l, conv_w, cos, sin):
    b = z_l.shape[0]
    n_hp = DN_PAIRS

    def part(tx, off):
        return pl.BlockSpec((1, tx, LANES), lambda bi, hp: (bi, 0, off + hp))

    def wpart(off):
        return pl.BlockSpec((DN_CONV, LANES), lambda bi, hp: (0, off + hp))

    tbl = pl.BlockSpec((SEQ, LANES), lambda bi, hp: (0, 0))
    per_head = pl.BlockSpec((1, 2, DN_T, LANES), lambda bi, hp: (bi, hp, 0, 0))
    return pl.pallas_call(
        _dnprep_kernel, grid=(b, n_hp),
        in_specs=[part(CTX_LEN, 0), part(CTX_LEN, n_hp), part(CTX_LEN, 2 * n_hp),
                  part(SEQ, 0), part(SEQ, n_hp), part(SEQ, 2 * n_hp),
                  wpart(0), wpart(n_hp), wpart(2 * n_hp), tbl, tbl],
        out_specs=[per_head, per_head, pl.BlockSpec((1, 1, LANES, DN_T), lambda bi, hp: (bi, hp, 0, 0))],
        out_shape=[jax.ShapeDtypeStruct((b, DN_HEADS, DN_T, LANES), F32),
                   jax.ShapeDtypeStruct((b, DN_HEADS, DN_T, LANES), F32),
                   jax.ShapeDtypeStruct((b, n_hp, LANES, DN_T), F32)],
        compiler_params=_params(("parallel", "parallel"), VMEM_LIMIT), name="dn_prepare",
    )(z_c, z_c, z_c, z_l, z_l, z_l, conv_w, conv_w, conv_w, cos, sin)


def _gate_select_matrices():
    sel = np.zeros((DN_PAIRS, LANES, LANES), np.float32)
    for hp in range(DN_PAIRS):
        for d in range(2):
            for hh in range(2):
                src = d * DN_HEADS + 2 * hp + hh
                sel[hp, src, 2 * d + hh] = 1.0
                sel[hp, 2 * DN_HEADS + src, 4 + 2 * d + hh] = 1.0
    return jnp.asarray(sel)


def _softplus(x):
    return jnp.maximum(x, 0.0) + jnp.log1p(jnp.exp(-jnp.abs(x)))


def _gates_kernel(zc_ref, zl_ref, nega_ref, dtb_ref, sel_ref, gc_ref, gr_ref):
    z = jnp.concatenate([zc_ref[0], zl_ref[0]], axis=0)
    lane = lax.broadcasted_iota(jnp.int32, (DN_CHUNK, LANES), 1)
    ri = lax.broadcasted_iota(jnp.int32, (DN_CHUNK, DN_CHUNK), 0)
    ci = lax.broadcasted_iota(jnp.int32, (DN_CHUNK, DN_CHUNK), 1)
    lower = (ri >= ci).astype(F32)
    upper = (ri <= ci).astype(F32)
    beta = jax.nn.sigmoid(z)
    g = nega_ref[...] * _softplus(z + dtb_ref[...])
    tiles = []
    for c in range(DN_NCHUNK):
        rows = slice(c * DN_CHUNK, (c + 1) * DN_CHUNK)
        gc = g[rows]
        prefix = jnp.dot(lower, gc, precision=HIGHEST, preferred_element_type=F32)
        suffix = jnp.dot(upper, gc, precision=HIGHEST, preferred_element_type=F32)
        tiles.append(jnp.where(lane < 2 * DN_HEADS, beta[rows], jnp.where(lane < 3 * DN_HEADS, prefix, suffix)))
    tile = jnp.concatenate(tiles, axis=0)
    for hp in range(DN_PAIRS):
        cols = jnp.dot(tile, sel_ref[hp], precision=HIGHEST, preferred_element_type=F32)
        gc_ref[0, hp] = cols
        gr_ref[0, hp] = cols.T[0:8, :]


def _dn_gates(z_c, z_l, neg_a, dt_bias, sel):
    b = z_l.shape[0]
    return pl.pallas_call(
        _gates_kernel, grid=(b,),
        in_specs=[pl.BlockSpec((1, CTX_LEN, LANES), lambda bi: (bi, 0, 0)),
                  pl.BlockSpec((1, SEQ, LANES), lambda bi: (bi, 0, 0)),
                  pl.BlockSpec((1, LANES), lambda bi: (0, 0)),
                  pl.BlockSpec((1, LANES), lambda bi: (0, 0)),
                  pl.BlockSpec(sel.shape, lambda bi: (0, 0, 0))],
        out_specs=[pl.BlockSpec((1, DN_PAIRS, DN_T, LANES), lambda bi: (bi, 0, 0, 0)),
                   pl.BlockSpec((1, DN_PAIRS, 8, DN_T), lambda bi: (bi, 0, 0, 0))],
        out_shape=[jax.ShapeDtypeStruct((b, DN_PAIRS, DN_T, LANES), F32),
                   jax.ShapeDtypeStruct((b, DN_PAIRS, 8, DN_T), F32)],
        compiler_params=_params(("parallel",), VMEM_LIMIT), name="dn_gates",
    )(z_c, z_l, neg_a, dt_bias, sel)


def _dot_bf16(a, b):
    return jnp.dot(a.astype(BF16), b.astype(BF16), preferred_element_type=F32)


def _block_masks():
    ri = lax.broadcasted_iota(jnp.int32, (LANES, LANES), 0)
    ci = lax.broadcasted_iota(jnp.int32, (LANES, LANES), 1)
    same = (ri // DN_CHUNK) == (ci // DN_CHUNK)
    rin = ri % DN_CHUNK
    cin = ci % DN_CHUNK
    incl = (same & (rin >= cin), same & (rin <= cin))
    strict = (same & (rin > cin), same & (rin < cin))
    return same, incl, strict


def _scan_phase_a(p, vk_ref, qp_ref, kt_ref, gc_ref, gr_ref, lws_ref, lfin_ref, gl_ref):
    c = DN_CHUNK
    same, incl, strict = _block_masks()
    other = jnp.logical_not(same)
    rows = pl.ds(pl.multiple_of(p * LANES, LANES), LANES)
    vk_a = vk_ref[0, 0, rows, :]
    vk_b = vk_ref[0, 1, rows, :]
    qp_a = qp_ref[0, 0, rows, :]
    qp_b = qp_ref[0, 1, rows, :]
    kts = kt_ref[0, 0, :, rows]
    ktr = pltpu.roll(kts, c, axis=1)
    gct = gc_ref[0, 0, rows, :]
    grt = gr_ref[0, 0, :, rows]
    grr = pltpu.roll(grt, c, axis=1)
    lane8 = lax.broadcasted_iota(jnp.int32, (8, LANES), 1)
    left8 = lane8 < c

    items = []
    for par in range(2):
        hs = slice(par * c, (par + 1) * c)
        vks = jnp.concatenate([vk_a[hs], vk_b[hs]], axis=0)
        qq = jnp.concatenate([qp_a[hs], qp_b[hs]], axis=0)
        kk_in = jnp.where(other, vks, 0.0)
        kk = lax.dot_general(jnp.concatenate([kk_in, qq], axis=0).astype(BF16), kk_in.astype(BF16),
                             (((1,), (1,)), ((), ())), preferred_element_type=F32)
        kt_raw = (jnp.concatenate([ktr[:c], kts[c:]], axis=0) if par == 0
                  else jnp.concatenate([kts[:c], ktr[c:]], axis=0))
        for d in range(2):
            bcol = lambda col: jnp.broadcast_to(gct[hs, col:col + 1], (c, LANES))
            beta = jnp.concatenate([bcol(2 * d), bcol(2 * d + 1)], axis=0)
            g_rows = jnp.concatenate([bcol(4 + 2 * d), bcol(5 + 2 * d)], axis=0)
            if par == 0:
                g_lane = jnp.where(left8, grt[4 + 2 * d:5 + 2 * d], grr[5 + 2 * d:6 + 2 * d])
            else:
                g_lane = jnp.where(left8, grr[4 + 2 * d:5 + 2 * d], grt[5 + 2 * d:6 + 2 * d])
            last = par * c + (c - 1 if d == 0 else 0)
            gl_a = jnp.broadcast_to(grt[4 + 2 * d:5 + 2 * d, last:last + 1], (8, LANES))
            gl_b = jnp.broadcast_to(grt[5 + 2 * d:6 + 2 * d, last:last + 1], (8, LANES))
            g_last = jnp.where(left8, gl_a, gl_b)
            g_cols = jnp.broadcast_to(g_lane[0:1], (LANES, LANES))
            decay = jnp.where(incl[d], jnp.exp(jnp.where(incl[d], g_rows - g_cols, 0.0)), 0.0)
            a = jnp.where(strict[d], kk[:LANES] * decay, 0.0) * beta
            qk = kk[LANES:] * decay
            eg = jnp.exp(g_rows)
            x = vks * beta * jnp.where(other, eg, 1.0)
            qe = qq * eg
            kt_s = jnp.where(other, kt_raw * jnp.exp(jnp.broadcast_to(g_last[0:1] - g_lane[0:1], (LANES, LANES))),
                             0.0)
            idx = (2 * p + par) * 2 + d
            lws_ref[idx, LANES:, :] = qe.astype(BF16)
            lfin_ref[idx] = jnp.concatenate([qk, kt_s], axis=0).astype(BF16)
            gl_ref[idx] = jnp.exp(g_last)
            items.append(dict(idx=idx, d=d, a=a, x=x))
    return items


def _scan_solve(items, xs_ref, lws_ref):
    ri = lax.broadcasted_iota(jnp.int32, (LANES, LANES), 0)
    ci = lax.broadcasted_iota(jnp.int32, (LANES, LANES), 1)
    eye = (ri == ci).astype(F32)
    base = DN_INV_BASE

    def off_block(size, d):
        row_second = (ri // size) % 2 == 1
        col_second = (ci // size) % 2 == 1
        inside = (ri // (2 * size)) == (ci // (2 * size))
        return inside & ((row_second & ~col_second) if d == 0 else (~row_second & col_second))

    for it in items:
        it["a8"] = jnp.where((ri // base) == (ci // base), it["a"], 0.0)
        it["t"] = eye - it["a8"]
    for it in items:
        it["p"] = _dot_bf16(it["a8"], it["a8"])
    for it in items:
        r = _dot_bf16(it["p"], jnp.concatenate([it["p"], it["t"]], axis=1))
        it["p"] = r[:, :LANES]
        it["t"] = it["t"] + r[:, LANES:]
    for it in items:
        it["t"] = it["t"] + _dot_bf16(it["p"], it["t"])
    size = base
    while size < DN_CHUNK:
        for it in items:
            it["m"] = _dot_bf16(jnp.where(off_block(size, it["d"]), it["a"], 0.0), it["t"])
        for it in items:
            it["t"] = it["t"] - _dot_bf16(it["t"], it["m"])
        size *= 2
    for it in items:
        it["x"] = _dot_bf16(it["t"], it["x"])
    for it in items:
        xs_ref[it["idx"]] = it["x"]
        lws_ref[it["idx"], :LANES, :] = it["x"].astype(BF16)


def _scan_phase_b(lo, hi, states, xs_ref, lws_ref, lfin_ref, gl_ref, out_refs):
    c = DN_CHUNK
    same, _, _ = _block_masks()
    other = jnp.logical_not(same)
    left = lax.broadcasted_iota(jnp.int32, (c, LANES), 1) < c

    def body(s, carry):
        u_f = s
        u_b = jnp.where(s < DN_CTX_CHUNKS, DN_CTX_CHUNKS - 1 - s, DN_NCHUNK + DN_CTX_CHUNKS - 1 - s)
        us = (u_f, u_b)
        idx = [us[d] * 2 + d for d in range(2)]
        res = [jnp.dot(lws_ref[idx[d]], carry[d].astype(BF16), preferred_element_type=F32) for d in range(2)]
        vn = [(xs_ref[idx[d]] - res[d][:LANES]).astype(BF16) for d in range(2)]
        fin = [jnp.dot(lfin_ref[idx[d]], vn[d], preferred_element_type=F32) for d in range(2)]
        new = []
        for d in range(2):
            if out_refs is not None:
                o = jnp.where(left, fin[d][:c] + res[d][LANES:LANES + c],
                              fin[d][c:LANES] + res[d][LANES + c:])
                out_refs[d][0, pl.ds(pl.multiple_of((us[d] - DN_CTX_CHUNKS) * c, c), c), :] = o
            decay = jnp.broadcast_to(gl_ref[idx[d]][0:1], (LANES, LANES))
            new.append(carry[d] * decay + jnp.where(other, fin[d][LANES:], 0.0))
        return tuple(new)

    return lax.fori_loop(lo, hi, body, states)


def _scan_kernel(vk_ref, qp_ref, kt_ref, gc_ref, gr_ref, of_ref, ob_ref, xs_ref, lws_ref, lfin_ref, gl_ref):
    def phase_a(step, carry):
        items = []
        for j in range(DN_PAIRS_PER_STEP):
            items += _scan_phase_a(step * DN_PAIRS_PER_STEP + j, vk_ref, qp_ref, kt_ref, gc_ref, gr_ref,
                                   lws_ref, lfin_ref, gl_ref)
        _scan_solve(items, xs_ref, lws_ref)
        return carry

    lax.fori_loop(0, DN_T // LANES // DN_PAIRS_PER_STEP, phase_a, 0)
    zero = jnp.zeros((LANES, LANES), F32)
    scratch = (xs_ref, lws_ref, lfin_ref, gl_ref)
    st = _scan_phase_b(0, DN_CTX_CHUNKS, (zero, zero), *scratch, None)
    _scan_phase_b(DN_CTX_CHUNKS, DN_NCHUNK, st, *scratch, (of_ref, ob_ref))


def _dn_scan(vk, qp, kt, g_cols, g_rows):
    b = vk.shape[0]
    n_items = 2 * DN_NCHUNK
    out = pl.BlockSpec((1, SEQ, LANES), lambda bi, hp: (bi, 0, hp))
    return pl.pallas_call(
        _scan_kernel, grid=(b, DN_PAIRS),
        in_specs=[pl.BlockSpec((1, 2, DN_T, LANES), lambda bi, hp: (bi, hp, 0, 0)),
                  pl.BlockSpec((1, 2, DN_T, LANES), lambda bi, hp: (bi, hp, 0, 0)),
                  pl.BlockSpec((1, 1, LANES, DN_T), lambda bi, hp: (bi, hp, 0, 0)),
                  pl.BlockSpec((1, 1, DN_T, LANES), lambda bi, hp: (bi, hp, 0, 0)),
                  pl.BlockSpec((1, 1, 8, DN_T), lambda bi, hp: (bi, hp, 0, 0))],
        out_specs=[out, out],
        out_shape=[jax.ShapeDtypeStruct((b, SEQ, DN_WIDTH), F32)] * 2,
        scratch_shapes=[pltpu.VMEM((n_items, LANES, LANES), F32),
                        pltpu.VMEM((n_items, 2 * LANES, LANES), BF16),
                        pltpu.VMEM((n_items, 2 * LANES, LANES), BF16),
                        pltpu.VMEM((n_items, 8, LANES), F32)],
        compiler_params=_params(("parallel", "parallel"), VMEM_LIMIT), name="dn_scan",
    )(vk, qp, kt, g_cols, g_rows)


POST_TM = 256
ROW_TILE = 8


def _store_row_tiles(ref, val):
    m = val.shape[0]
    for j in range(ROW_TILE):
        ref[pl.ds(j, m, stride=ROW_TILE), :] = val[:, j * LANES:(j + 1) * LANES]


def _load_row_tiles(ref, first, m):
    return jnp.concatenate([ref[pl.ds(first * ROW_TILE + j, m, stride=ROW_TILE), :] for j in range(ROW_TILE)],
                           axis=1)


def _post_kernel(of_ref, ob_ref, dg_ref, ona_ref, gab_ref, x_ref, g1_ref, sh2_ref, sc2_ref,
                 dnw_ref, wa_ref, wb_ref, wo_ref, n2w_ref, wr_ref, br_ref,
                 xl_ref, h2_ref, route_ref, cnt_ref):
    i = pl.program_id(0)

    @pl.when(i == 0)
    def _():
        cnt_ref[...] = jnp.zeros_like(cnt_ref)

    o = of_ref[...] + ob_ref[...]
    hi = lax.broadcasted_iota(jnp.int32, (DN_WIDTH, DN_WIDTH), 0) // DN_HEAD_DIM
    hj = lax.broadcasted_iota(jnp.int32, (DN_WIDTH, DN_WIDTH), 1) // DN_HEAD_DIM
    head_mean = jnp.where(hi == hj, 1.0 / DN_HEAD_DIM, 0.0).astype(BF16)
    ms = _dot_split_lhs(o * o, head_mean)
    dg = dg_ref[...]
    o_dn = (o * lax.rsqrt(ms + NORM_EPS) * dnw_ref[...]) * (dg * jax.nn.sigmoid(dg))
    gab = gab_ref[...]
    ya = jnp.dot(ona_ref[...], wa_ref[...], preferred_element_type=F32)
    yb = jnp.dot(o_dn.astype(BF16), wb_ref[...], preferred_element_type=F32)
    y = jax.nn.sigmoid(gab[:, :D_MODEL]) * ya + jax.nn.sigmoid(gab[:, D_MODEL:]) * yb
    y = jnp.dot(y.astype(BF16), wo_ref[...], preferred_element_type=F32)
    xl = x_ref[...] + g1_ref[0] * y
    xl_ref[...] = xl
    ms2 = jnp.mean(xl * xl, axis=-1, keepdims=True)
    h2 = (xl * lax.rsqrt(ms2 + NORM_EPS) * n2w_ref[...]) * (1.0 + sc2_ref[0]) + sh2_ref[0]
    _store_row_tiles(h2_ref, h2)

    h_hi, h_lo = _split_bf16(h2)
    logits = (jnp.dot(h_hi, wr_ref[0], preferred_element_type=F32) + jnp.dot(h_lo, wr_ref[0], preferred_element_type=F32)
              + jnp.dot(h_hi, wr_ref[1], preferred_element_type=F32) + br_ref[...])
    tm = logits.shape[0]
    lane = lax.broadcasted_iota(jnp.int32, (tm, LANES), 1).astype(F32)
    vals, idxs = [], []
    cur = logits
    for _ in range(TOP_K):
        m = jnp.max(cur, axis=-1, keepdims=True)
        idx = jnp.min(jnp.where(cur == m, lane, float(LANES)), axis=-1, keepdims=True)
        vals.append(m)
        idxs.append(idx)
        cur = jnp.where(lane == idx, -jnp.inf, cur)
    es = [jnp.exp(v - vals[0]) for v in vals]
    den = es[0] + es[1] + es[2] + es[3]
    onehot = jnp.zeros((tm, LANES), F32)
    for idx in idxs:
        onehot = onehot + jnp.where(lane == idx, 1.0, 0.0)
    ti = lax.broadcasted_iota(jnp.int32, (tm, tm), 0)
    tj = lax.broadcasted_iota(jnp.int32, (tm, tm), 1)
    before = jnp.where(ti > tj, 1.0, 0.0).astype(BF16)
    cnt = cnt_ref[...] + jnp.dot(before, onehot.astype(BF16), preferred_element_type=F32)
    route = jnp.zeros((tm, LANES), F32)
    for kk in range(TOP_K):
        rank = jnp.sum(jnp.where(lane == idxs[kk], cnt, 0.0), axis=-1, keepdims=True)
        route = jnp.where(lane == float(kk), es[kk] / den, route)
        route = jnp.where(lane == float(TOP_K + kk), idxs[kk], route)
        route = jnp.where(lane == float(2 * TOP_K + kk), rank, route)
    route_ref[...] = route
    cnt_ref[...] = cnt_ref[...] + jnp.sum(onehot, axis=0, keepdims=True)


def _post_mixer(o_f, o_b, dg, o_na, gab, x, g1, sh2, sc2, dn_norm_w, w_br_a, w_br_b, w_out, norm2_w, wr, br):
    n, d = x.shape
    tm = POST_TM
    per_batch = SEQ // tm
    tok = lambda w: pl.BlockSpec((tm, w), lambda i: (i, 0))
    row = pl.BlockSpec((1, 1, d), lambda i: (i // per_batch, 0, 0))
    const = lambda a: pl.BlockSpec(a.shape, lambda i: (0,) * a.ndim)
    return pl.pallas_call(
        _post_kernel, grid=(n // tm,),
        in_specs=[tok(DN_WIDTH), tok(DN_WIDTH), tok(DN_WIDTH), tok(NA_WIDTH), tok(2 * D_MODEL), tok(d),
                  row, row, row, const(dn_norm_w), const(w_br_a), const(w_br_b), const(w_out),
                  const(norm2_w), const(wr), const(br)],
        out_specs=[tok(d), pl.BlockSpec((tm * ROW_TILE, LANES), lambda i: (i, 0)), tok(LANES),
                   pl.BlockSpec((1, LANES), lambda i: (0, 0))],
        out_shape=[jax.ShapeDtypeStruct((n, d), F32), jax.ShapeDtypeStruct((n * ROW_TILE, LANES), F32),
                   jax.ShapeDtypeStruct((n, LANES), F32), jax.ShapeDtypeStruct((1, LANES), F32)],
        compiler_params=_params(("arbitrary",), VMEM_LIMIT), name="post_mixer_router",
    )(o_f, o_b, dg, o_na, gab, x, g1, sh2, sc2, dn_norm_w, w_br_a, w_br_b, w_out, norm2_w, wr, br)


SLOT_CHUNK = 8192
SLOT_UNROLL = 8


def _slot_map_kernel(lo_ref, hi_ref, dest_ref, tok_ref):
    c = pl.program_id(0)

    @pl.when(c == 0)
    def _():
        def fill_range(r, carry):
            def fill(s, inner):
                tok_ref[s] = 0
                return inner
            return lax.fori_loop(lo_ref[r], hi_ref[r], fill, carry)
        lax.fori_loop(0, N_EXPERTS + 1, fill_range, 0)

    def put(i, carry):
        for j in range(SLOT_UNROLL):
            a = i * SLOT_UNROLL + j
            tok_ref[dest_ref[0, 0, a]] = lax.shift_right_logical(c * SLOT_CHUNK + a, TOP_K.bit_length() - 1)
        return carry
    lax.fori_loop(0, SLOT_CHUNK // SLOT_UNROLL, put, 0)


def _slot_map(pad_lo, pad_hi, dest):
    grid_spec = pltpu.PrefetchScalarGridSpec(
        num_scalar_prefetch=2, grid=(N_ASG // SLOT_CHUNK,),
        in_specs=[pl.BlockSpec((1, 1, SLOT_CHUNK), lambda c, lo, hi: (c, 0, 0), memory_space=pltpu.SMEM)],
        out_specs=pl.BlockSpec((MOE_MPAD,), lambda c, lo, hi: (0,), memory_space=pltpu.SMEM))
    return pl.pallas_call(
        _slot_map_kernel, grid_spec=grid_spec,
        out_shape=jax.ShapeDtypeStruct((MOE_MPAD,), jnp.int32),
        compiler_params=_params(("arbitrary",)), name="slot_map",
    )(pad_lo, pad_hi, dest.reshape(N_ASG // SLOT_CHUNK, 1, SLOT_CHUNK))


MOE_LOOKAHEAD = 2
MOE_SLOTS = MOE_LOOKAHEAD + 1
DMA_ISSUE_UNROLL = 8
DMA_WAIT_UNROLL = 32


def _row_copy(src_hbm, dst_vmem, src_row, dst_row, sem):
    src = pl.ds(pl.multiple_of(src_row * ROW_TILE, ROW_TILE), ROW_TILE)
    dst = pl.ds(pl.multiple_of(dst_row * ROW_TILE, ROW_TILE), ROW_TILE)
    return pltpu.make_async_copy(src_hbm.at[src], dst_vmem.at[dst], sem)


def _expert_kernel(blk_e_ref, nact_ref, tok_ref, par_ref, nxt_ref, h_hbm, w1_hbm, b1_ref, w2_hbm, b2_ref, y_ref,
                   xbuf, w1s, w2s, w1b, w2b, sems, wsems):
    i = pl.program_id(0)
    n_active = nact_ref[0]
    bm = MOE_BM

    def issue(blk, slot):
        def body(g, carry):
            for j in range(DMA_ISSUE_UNROLL):
                s = g * DMA_ISSUE_UNROLL + j
                _row_copy(h_hbm, xbuf.at[slot], tok_ref[blk * bm + s], s, sems.at[slot]).start()
            return carry
        lax.fori_loop(0, bm // DMA_ISSUE_UNROLL, body, 0)

    def wait(slot):
        def body(g, carry):
            for j in range(DMA_WAIT_UNROLL):
                _row_copy(h_hbm, xbuf.at[slot], 0, g * DMA_WAIT_UNROLL + j, sems.at[slot]).wait()
            return carry
        lax.fori_loop(0, bm // DMA_WAIT_UNROLL, body, 0)

    def weight_copies(e, p):
        return (pltpu.make_async_copy(w1_hbm.at[e], w1s.at[p], wsems.at[p]),
                pltpu.make_async_copy(w2_hbm.at[e], w2s.at[p], wsems.at[p]))

    @pl.when(i == 0)
    def _():
        for cp in weight_copies(blk_e_ref[0], par_ref[0]):
            cp.start()
        for j in range(MOE_LOOKAHEAD):
            issue(jnp.minimum(j, n_active - 1), j)

    changed = jnp.logical_or(i == 0, blk_e_ref[i] != blk_e_ref[jnp.maximum(i - 1, 0)])

    @pl.when(jnp.logical_and(i < n_active, changed))
    def _():
        p = par_ref[i]
        for cp in weight_copies(blk_e_ref[i], p):
            cp.wait()
        w1b[...] = w1s[p].astype(BF16)
        w2b[...] = w2s[p].astype(BF16)

        @pl.when(nxt_ref[i] >= 0)
        def _():
            for cp in weight_copies(nxt_ref[i], 1 - p):
                cp.start()

    @pl.when(i < n_active)
    def _():
        slot = i % MOE_SLOTS
        wait(slot)
        xb = _load_row_tiles(xbuf.at[slot], 0, bm).astype(BF16)
        nxt = jnp.minimum(i + MOE_LOOKAHEAD, n_active - 1)
        nslot = (i + MOE_LOOKAHEAD) % MOE_SLOTS
        for s in range(bm):
            _row_copy(h_hbm, xbuf.at[nslot], tok_ref[nxt * bm + s], s, sems.at[nslot]).start()
        hb = jnp.dot(xb, w1b[...], preferred_element_type=F32) + b1_ref[0]
        gate = jnp.minimum(hb[:, :D_EXPERT], SWIGLU_LIMIT)
        up = jnp.clip(hb[:, D_EXPERT:], -SWIGLU_LIMIT, SWIGLU_LIMIT)
        act = (up + 1.0) * gate * jax.nn.sigmoid(SWIGLU_ALPHA * gate)
        _store_row_tiles(y_ref, jnp.dot(act.astype(BF16), w2b[...], preferred_element_type=F32) + b2_ref[0])

    @pl.when(i == n_active)
    def _():
        for j in range(MOE_LOOKAHEAD):
            wait((i + j) % MOE_SLOTS)

    @pl.when(i >= n_active)
    def _():
        y_ref[...] = jnp.zeros_like(y_ref)


def _expert_ffn(blk_expert, n_active, tok_pad, stage_slot, next_expert, h2, w1, b1, w2, b2):
    d = w1.shape[1]
    f2 = w1.shape[2]

    def live(i, nact):
        return jnp.minimum(i, jnp.maximum(nact[0] - 1, 0))

    bias = lambda w: pl.BlockSpec((1, 1, w), lambda i, be, na, tk, sp, ne: (be[live(i, na)], 0, 0))
    grid_spec = pltpu.PrefetchScalarGridSpec(
        num_scalar_prefetch=5, grid=(MOE_NBLK,),
        in_specs=[pl.BlockSpec(memory_space=pl.ANY), pl.BlockSpec(memory_space=pl.ANY), bias(f2),
                  pl.BlockSpec(memory_space=pl.ANY), bias(d)],
        out_specs=pl.BlockSpec((MOE_BM * ROW_TILE, LANES), lambda i, be, na, tk, sp, ne: (i, 0)),
        scratch_shapes=[pltpu.VMEM((MOE_SLOTS, MOE_BM * ROW_TILE, LANES), F32),
                        pltpu.VMEM((2, d, f2), F32), pltpu.VMEM((2, f2 // 2, d), F32),
                        pltpu.VMEM((d, f2), BF16), pltpu.VMEM((f2 // 2, d), BF16),
                        pltpu.SemaphoreType.DMA((MOE_SLOTS,)), pltpu.SemaphoreType.DMA((2,))])
    return pl.pallas_call(
        _expert_kernel, grid_spec=grid_spec,
        out_shape=jax.ShapeDtypeStruct((MOE_MPAD * ROW_TILE, LANES), F32),
        compiler_params=_params(("arbitrary",), VMEM_LIMIT), name="expert_ffn",
    )(blk_expert, n_active, tok_pad, stage_slot, next_expert, h2, w1, b1, w2, b2)


COMB_TM = 128


def _combine_kernel(dest_ref, y_hbm, xl_ref, route_ref, g2_ref, fw_ref, o_ref, ybuf, sems):
    i = pl.program_id(0)
    n_steps = pl.num_programs(0)
    tm = COMB_TM

    def issue(step, slot):
        def body(g, carry):
            for j in range(DMA_ISSUE_UNROLL // TOP_K):
                t = g * (DMA_ISSUE_UNROLL // TOP_K) + j
                for kk in range(TOP_K):
                    _row_copy(y_hbm, ybuf.at[slot], dest_ref[(step * tm + t) * TOP_K + kk], kk * tm + t,
                              sems.at[slot]).start()
            return carry
        lax.fori_loop(0, tm * TOP_K // DMA_ISSUE_UNROLL, body, 0)

    def wait(slot):
        def body(g, carry):
            for j in range(DMA_WAIT_UNROLL):
                _row_copy(y_hbm, ybuf.at[slot], 0, g * DMA_WAIT_UNROLL + j, sems.at[slot]).wait()
            return carry
        lax.fori_loop(0, tm * TOP_K // DMA_WAIT_UNROLL, body, 0)

    @pl.when(i == 0)
    def _():
        issue(0, 0)

    @pl.when(i + 1 < n_steps)
    def _():
        issue(i + 1, (i + 1) % 2)

    slot = i % 2
    wait(slot)
    route = route_ref[...]
    moe = jnp.zeros((tm, D_MODEL), F32)
    for kk in range(TOP_K):
        moe = moe + route[:, kk:kk + 1] * _load_row_tiles(ybuf.at[slot], kk * tm, tm)
    xo = xl_ref[...] + g2_ref[0] * moe
    ms = jnp.mean(xo * xo, axis=-1, keepdims=True)
    o_ref[...] = xo * lax.rsqrt(ms + NORM_EPS) * fw_ref[...]


def _combine(dest, y_sorted, xl, route, g2, final_w):
    n, d = xl.shape
    tm = COMB_TM
    per_batch = SEQ // tm
    grid_spec = pltpu.PrefetchScalarGridSpec(
        num_scalar_prefetch=1, grid=(n // tm,),
        in_specs=[pl.BlockSpec(memory_space=pl.ANY),
                  pl.BlockSpec((tm, d), lambda i, ds: (i, 0)),
                  pl.BlockSpec((tm, LANES), lambda i, ds: (i, 0)),
                  pl.BlockSpec((1, 1, d), lambda i, ds: (i // per_batch, 0, 0)),
                  pl.BlockSpec((1, d), lambda i, ds: (0, 0))],
        out_specs=pl.BlockSpec((tm, d), lambda i, ds: (i, 0)),
        scratch_shapes=[pltpu.VMEM((2, TOP_K * tm * ROW_TILE, LANES), F32), pltpu.SemaphoreType.DMA((2,))])
    return pl.pallas_call(
        _combine_kernel, grid_spec=grid_spec,
        out_shape=jax.ShapeDtypeStruct((n, d), F32),
        compiler_params=_params(("arbitrary",), VMEM_LIMIT), name="moe_combine",
    )(dest, y_sorted, xl, route, g2, final_w)


def _split_in_weights(w_in):
    o = np.cumsum((0, NA_WIDTH, NA_WIDTH, NA_WIDTH, 3 * DN_WIDTH, DN_WIDTH, 2 * DN_HEADS, 2 * DN_HEADS,
                   D_MODEL, D_MODEL))
    wb = w_in.astype(BF16)
    w_na = wb[:, o[0]:o[3]]
    w_dq = wb[:, o[3]:o[4]]
    w_dg = wb[:, o[4]:o[5]]
    w_gab = wb[:, o[7]:o[9]]
    w_dba = jnp.pad(wb[:, o[5]:o[7]], ((0, 0), (0, LANES - 4 * DN_HEADS)))
    return w_na, w_dq, w_dg, w_gab, w_dba


def kernel(x, c, ctx, c_ctx, w_mod, b_mod, norm1_w, w_in, na_rpb, dn_conv_w, dn_a_log, dn_dt_bias, dn_norm_w,
           w_br_a, w_br_b, w_out, norm2_w, w_router, b_router, w1, b1, w2, b2, final_norm_w):
    d = D_MODEL
    cc = jnp.concatenate([c, c_ctx[None], jnp.zeros((16 - BATCH - 1, d), F32)], axis=0)
    mod = _modulation(cc, w_mod[0], b_mod[0][None])
    mod_l = mod[:BATCH].reshape(BATCH, 6, 1, d)
    sh1, sc1, g1, sh2, sc2, g2 = (mod_l[:, i] for i in range(6))
    mod_c = jnp.broadcast_to(mod[BATCH].reshape(6, 1, 1, d), (6, BATCH, 1, d))

    weights = _split_in_weights(w_in[0])
    n1w = norm1_w[0][None]
    naq, nak, nav, dqkv_l, dg_l, gab_l, dba_l = _in_projection(x, sh1, sc1, n1w, weights, 512)
    _, nak_c, nav_c, dqkv_c, _, _, dba_c = _in_projection(ctx, mod_c[0], mod_c[1], n1w, weights, CTX_LEN)

    o_na = _neighborhood_attention(naq, nak, nav, nak_c, nav_c, _na_bias_table(na_rpb[0]))

    cos, sin = _rope_tables(SEQ)
    vk, qp, kt = _dn_prepare(dqkv_c, dqkv_l, dn_conv_w[0], cos, sin)
    pad16 = lambda v: jnp.pad(v.reshape(1, 2 * DN_HEADS), ((0, 0), (2 * DN_HEADS, LANES - 4 * DN_HEADS)))
    g_cols, g_rows = _dn_gates(dba_c, dba_l, pad16(-jnp.exp(dn_a_log[0])), pad16(dn_dt_bias[0]),
                               _gate_select_matrices())
    o_f, o_b = _dn_scan(vk, qp, kt, g_cols, g_rows)

    flat = lambda a: a.reshape(N_TOK, a.shape[-1])
    wr = jnp.stack(_split_bf16(jnp.pad(w_router[0], ((0, 0), (0, LANES - N_EXPERTS)))))
    br = jnp.concatenate([b_router[0], jnp.full((LANES - N_EXPERTS,), NEG_INF, F32)])[None]
    dnw = jnp.tile(dn_norm_w[0], DN_HEADS)[None]
    xl, h2, route, counts = _post_mixer(
        flat(o_f), flat(o_b), flat(dg_l), flat(o_na), flat(gab_l), flat(x), g1, sh2, sc2, dnw,
        w_br_a[0].astype(BF16), w_br_b[0].astype(BF16), w_out[0].astype(BF16), norm2_w[0][None], wr, br)

    e_idx = route[:, TOP_K:2 * TOP_K].astype(jnp.int32)
    rank = route[:, 2 * TOP_K:3 * TOP_K].astype(jnp.int32)
    cnt = counts[0, :N_EXPERTS].astype(jnp.int32)
    padded = (cnt + MOE_BM - 1) // MOE_BM * MOE_BM
    pad_ends = jnp.cumsum(padded)
    pad_starts = pad_ends - padded
    dest = (pad_starts[e_idx] + rank).reshape(-1)
    total = jnp.full((1,), MOE_MPAD, jnp.int32)
    tok_pad = _slot_map(jnp.concatenate([pad_starts + cnt, pad_ends[-1:]]), jnp.concatenate([pad_ends, total]), dest)
    blk_first_row = jnp.arange(MOE_NBLK, dtype=jnp.int32) * MOE_BM
    blk_expert = jnp.minimum(jnp.sum((pad_ends[None, :] <= blk_first_row[:, None]).astype(jnp.int32), axis=1),
                             N_EXPERTS - 1)
    n_active = (pad_ends[-1:] // MOE_BM).astype(jnp.int32)

    new_group = jnp.concatenate([jnp.ones((1,), jnp.int32), (blk_expert[1:] != blk_expert[:-1]).astype(jnp.int32)])
    stage_slot = jnp.cumsum(new_group) % 2
    ids = jnp.arange(N_EXPERTS, dtype=jnp.int32)
    later = (ids[None, :] > ids[:, None]) & (cnt[None, :] > 0)
    next_nonempty = jnp.min(jnp.where(later, ids[None, :], N_EXPERTS), axis=1)
    next_expert = jnp.where(next_nonempty < N_EXPERTS, next_nonempty, -1)[blk_expert]

    y_sorted = _expert_ffn(blk_expert, n_active, tok_pad, stage_slot, next_expert, h2, w1[0], b1[0][:, None], w2[0],
                           b2[0][:, None])
    out = _combine(dest, y_sorted, xl, route, g2, final_norm_w[None])
    return out.reshape(BATCH, SEQ, d)
```

```python
import functools

import numpy as np
import jax
import jax.numpy as jnp
from jax import lax
from jax.experimental import pallas as pl
from jax.experimental.pallas import tpu as pltpu

F32 = jnp.float32
BF16 = jnp.bfloat16
HIGHEST = lax.Precision.HIGHEST

D_MODEL = 1024
BATCH = 8
SEQ = 2048
GRID_W = 64
GRID_ROWS = SEQ // GRID_W
CTX_LEN = 256
NA_HEADS = 8
NA_HEAD_DIM = 64
NA_KH = 8
NA_KW = 16
NA_SCALE = NA_HEAD_DIM ** -0.5
NA_WIDTH = NA_HEADS * NA_HEAD_DIM
DN_HEADS = 8
DN_HEAD_DIM = 64
DN_WIDTH = DN_HEADS * DN_HEAD_DIM
DN_CONV = 5
DN_CHUNK = 64
ROPE_THETA = 10000.0
N_EXPERTS = 32
TOP_K = 4
D_EXPERT = 1024
SWIGLU_LIMIT = 7.0
SWIGLU_ALPHA = 1.702
NORM_EPS = 1e-6
NEG_INF = -1e30

LANES = 128
N_TOK = BATCH * SEQ
N_ASG = N_TOK * TOP_K
MOE_BM = 256
MOE_MPAD = -(-(N_ASG + N_EXPERTS * (MOE_BM - 1)) // MOE_BM) * MOE_BM
MOE_NBLK = MOE_MPAD // MOE_BM
VMEM_LIMIT = 56 * 1024 * 1024

DN_T = CTX_LEN + SEQ
DN_NCHUNK = DN_T // DN_CHUNK
DN_CTX_CHUNKS = CTX_LEN // DN_CHUNK
DN_PAIRS = DN_HEADS // 2
DN_PAIRS_PER_STEP = 6
DN_INV_BASE = 8


def _params(sem, vmem=None):
    return pltpu.CompilerParams(dimension_semantics=sem, vmem_limit_bytes=vmem)


def _split_bf16(x):
    hi = x.astype(BF16)
    return hi, (x - hi.astype(F32)).astype(BF16)


def _dot_split_lhs(x, w_bf16):
    hi, lo = _split_bf16(x)
    return (jnp.dot(hi, w_bf16, preferred_element_type=F32) + jnp.dot(lo, w_bf16, preferred_element_type=F32))


def _mod_kernel(c_ref, w_ref, b_ref, o_ref):
    c = c_ref[...]
    s = c * jax.nn.sigmoid(c)
    o_ref[...] = jnp.dot(s, w_ref[...], precision=HIGHEST, preferred_element_type=F32) + b_ref[...]


def _modulation(cc, w_mod, b_mod):
    rows, d = cc.shape
    n = w_mod.shape[1]
    bn = 1024
    return pl.pallas_call(
        _mod_kernel, grid=(n // bn,),
        in_specs=[pl.BlockSpec((rows, d), lambda j: (0, 0)),
                  pl.BlockSpec((d, bn), lambda j: (0, j)),
                  pl.BlockSpec((1, bn), lambda j: (0, j))],
        out_specs=pl.BlockSpec((rows, bn), lambda j: (0, j)),
        out_shape=jax.ShapeDtypeStruct((rows, n), F32),
        compiler_params=_params(("parallel",)), name="modulation")(cc, w_mod, b_mod)


def _inproj_kernel(x_ref, sh_ref, sc_ref, nw_ref, wna_ref, wdq_ref, wdg_ref, wgab_ref, wdba_ref,
                   q_ref, k_ref, v_ref, dq_ref, dg_ref, gab_ref, dba_ref):
    x = x_ref[0]
    ms = jnp.mean(x * x, axis=-1, keepdims=True)
    y = x * lax.rsqrt(ms + NORM_EPS) * nw_ref[...]
    h = (y * (1.0 + sc_ref[0]) + sh_ref[0]).astype(BF16)
    na = jnp.dot(h, wna_ref[...], preferred_element_type=F32)
    q_ref[0] = (na[:, :NA_WIDTH] * NA_SCALE).astype(BF16)
    k_ref[0] = na[:, NA_WIDTH:2 * NA_WIDTH].astype(BF16)
    v_ref[0] = na[:, 2 * NA_WIDTH:].astype(BF16)
    dq_ref[0] = jnp.dot(h, wdq_ref[...], preferred_element_type=F32)
    dg_ref[0] = jnp.dot(h, wdg_ref[...], preferred_element_type=F32)
    gab_ref[0] = jnp.dot(h, wgab_ref[...], preferred_element_type=F32)
    dba_ref[0] = jnp.dot(h, wdba_ref[...], preferred_element_type=F32)


def _in_projection(x, shift, scale, norm_w, weights, tm):
    bx, tx, d = x.shape
    wna, wdq, wdg, wgab, wdba = weights
    tok = lambda w: pl.BlockSpec((1, tm, w), lambda b, i: (b, i, 0))
    row = pl.BlockSpec((1, 1, d), lambda b, i: (b, 0, 0))
    const = lambda a: pl.BlockSpec(a.shape, lambda b, i: (0,) * a.ndim)
    widths = (NA_WIDTH, NA_WIDTH, NA_WIDTH, 3 * DN_WIDTH, DN_WIDTH, 2 * D_MODEL, LANES)
    dtypes = (BF16, BF16, BF16, F32, F32, F32, F32)
    return pl.pallas_call(
        _inproj_kernel, grid=(bx, tx // tm),
        in_specs=[tok(d), row, row, const(norm_w), const(wna), const(wdq), const(wdg), const(wgab), const(wdba)],
        out_specs=[tok(w) for w in widths],
        out_shape=[jax.ShapeDtypeStruct((bx, tx, w), dt) for w, dt in zip(widths, dtypes)],
        compiler_params=_params(("parallel", "parallel"), VMEM_LIMIT), name="in_projection",
    )(x, shift, scale, norm_w, wna, wdq, wdg, wgab, wdba)


def _na_bias_table(rpb):
    col = np.arange(GRID_W)
    col_start = np.clip(col - NA_KW // 2, 0, GRID_W - NA_KW)
    mask = (col[None, :] >= col_start[:, None]) & (col[None, :] < col_start[:, None] + NA_KW)
    dc = np.clip(col[None, :] - col[:, None] + NA_KW - 1, 0, 2 * NA_KW - 2)
    pick = np.zeros((2 * NA_KW - 1, GRID_W * GRID_W), np.float32)
    pick[dc.reshape(-1), np.arange(GRID_W * GRID_W)] = 1.0
    n_dr = 2 * NA_KH - 1
    by_dr = jnp.dot(rpb.astype(F32).reshape(NA_HEADS * n_dr, 2 * NA_KW - 1), pick, precision=HIGHEST)
    by_dr = jnp.where(mask[None, None], by_dr.reshape(NA_HEADS, n_dr, GRID_W, GRID_W), NEG_INF)
    tbl = jnp.stack([by_dr[:, c:c + NA_KH] for c in range(NA_KH)], axis=0)
    return tbl.transpose(0, 1, 3, 2, 4).reshape(NA_KH, NA_HEADS, GRID_W, NA_KH * GRID_W)


NA_HEAD_GROUP = 4


def _na_first_row(r):
    return jnp.clip(r - NA_KH // 2, 0, GRID_ROWS - NA_KH)


def _na_kernel(q_ref, k_ref, v_ref, kc_ref, vc_ref, bias_ref, o_ref):
    r = pl.program_id(1)
    start = pl.multiple_of(_na_first_row(r) * GRID_W, GRID_W)
    q = q_ref[0]
    kw = k_ref[0, pl.ds(start, NA_KH * GRID_W), :]
    vw = v_ref[0, pl.ds(start, NA_KH * GRID_W), :]
    kc = kc_ref[0]
    vc = vc_ref[0]
    nt = (((1,), (1,)), ((), ()))
    outs = []
    first_head = lax.broadcasted_iota(jnp.int32, (GRID_W, LANES), 1) < NA_HEAD_DIM
    zero = jnp.zeros((), BF16)
    for g0 in range(0, NA_HEADS, NA_HEAD_GROUP):
        heads = range(g0, g0 + NA_HEAD_GROUP)
        blk = [slice((h // 2) * LANES, (h // 2 + 1) * LANES) for h in heads]
        qm = [jnp.where(first_head if h % 2 == 0 else ~first_head, q[:, sl], zero) for h, sl in zip(heads, blk)]
        s = [lax.dot_general(a, kw[:, sl], nt, preferred_element_type=F32) + bias_ref[0, h]
             for a, sl, h in zip(qm, blk, heads)]
        sc = [lax.dot_general(a, kc[:, sl], nt, preferred_element_type=F32) for a, sl in zip(qm, blk)]
        m = [jnp.maximum(jnp.max(a, axis=-1, keepdims=True), jnp.max(b, axis=-1, keepdims=True))
             for a, b in zip(s, sc)]
        p = [jnp.exp(a - mm) for a, mm in zip(s, m)]
        pc = [jnp.exp(b - mm) for b, mm in zip(sc, m)]
        denom = [jnp.sum(a, axis=-1, keepdims=True) + jnp.sum(b, axis=-1, keepdims=True) for a, b in zip(p, pc)]
        o = [(jnp.dot(a.astype(BF16), vw[:, sl], preferred_element_type=F32)
              + jnp.dot(b.astype(BF16), vc[:, sl], preferred_element_type=F32)) / dd
             for a, b, sl, dd in zip(p, pc, blk, denom)]
        outs += [jnp.where(first_head, o[i], o[i + 1]) for i in range(0, NA_HEAD_GROUP, 2)]
    o_ref[0] = jnp.concatenate(outs, axis=1).astype(BF16)


def _neighborhood_attention(q, k, v, kc, vc, bias):
    b, t, w = q.shape

    def bias_map(bi, r):
        return (_na_first_row(r) - r + NA_KH - 1, 0, 0, 0)

    return pl.pallas_call(
        _na_kernel, grid=(b, GRID_ROWS),
        in_specs=[pl.BlockSpec((1, GRID_W, w), lambda bi, r: (bi, r, 0)),
                  pl.BlockSpec((1, t, w), lambda bi, r: (bi, 0, 0)),
                  pl.BlockSpec((1, t, w), lambda bi, r: (bi, 0, 0)),
                  pl.BlockSpec((1, CTX_LEN, w), lambda bi, r: (bi, 0, 0)),
                  pl.BlockSpec((1, CTX_LEN, w), lambda bi, r: (bi, 0, 0)),
                  pl.BlockSpec((1, NA_HEADS, GRID_W, NA_KH * GRID_W), bias_map)],
        out_specs=pl.BlockSpec((1, GRID_W, w), lambda bi, r: (bi, r, 0)),
        out_shape=jax.ShapeDtypeStruct((b, t, w), BF16),
        compiler_params=_params(("parallel", "parallel"), VMEM_LIMIT), name="neighborhood_attention",
    )(q, k, v, kc, vc, bias)


def _rope_tables(t_len):
    t = jnp.arange(t_len)
    row = (t // GRID_W).astype(F32)
    col = (t % GRID_W).astype(F32)
    n_axis = DN_HEAD_DIM // 4
    freqs = ROPE_THETA ** (-jnp.arange(n_axis, dtype=F32) / n_axis)
    ang = jnp.concatenate([row[:, None] * freqs, col[:, None] * freqs], axis=-1)
    cos = jnp.tile(jnp.cos(ang), (1, 4))
    sin = jnp.sin(ang)
    sin_signed = jnp.tile(jnp.concatenate([-sin, sin], axis=-1), (1, 2))
    return cos, sin_signed


def _conv_silu(x, w):
    tx = x.shape[0]
    t = lax.broadcasted_iota(jnp.int32, x.shape, 0)
    acc = x * w[DN_CONV // 2:DN_CONV // 2 + 1, :]
    for tap in range(DN_CONV):
        shift = DN_CONV // 2 - tap
        if shift == 0:
            continue
        xs = pltpu.roll(x, shift % tx, axis=0)
        valid = (t >= shift) & (t < tx + shift)
        acc = acc + jnp.where(valid, xs, 0.0) * w[tap:tap + 1, :]
    return acc * jax.nn.sigmoid(acc)


def _dnprep_kernel(cq_ref, ck_ref, cv_ref, lq_ref, lk_ref, lv_ref, wq_ref, wk_ref, wv_ref, cos_ref, sin_ref,
                   vk_ref, qp_ref, kt_ref):
    li = lax.broadcasted_iota(jnp.int32, (LANES, LANES), 0) // DN_HEAD_DIM
    lj = lax.broadcasted_iota(jnp.int32, (LANES, LANES), 1) // DN_HEAD_DIM
    same_head = (li == lj).astype(F32)
    half = DN_HEAD_DIM // 2

    def l2norm(y):
        return y * lax.rsqrt(_dot_split_lhs(y * y, same_head.astype(BF16)) + NORM_EPS)

    def rope(y):
        lane = lax.broadcasted_iota(jnp.int32, y.shape, 1)
        partner = jnp.where(lane % DN_HEAD_DIM < half,
                            pltpu.roll(y, LANES - half, axis=1), pltpu.roll(y, half, axis=1))
        return y * cos_ref[...] + partner * sin_ref[...]

    off = 0
    for q_ref, k_ref, v_ref, use_rope in ((cq_ref, ck_ref, cv_ref, False), (lq_ref, lk_ref, lv_ref, True)):
        q = l2norm(_conv_silu(q_ref[0], wq_ref[...]))
        k = l2norm(_conv_silu(k_ref[0], wk_ref[...]))
        v = _conv_silu(v_ref[0], wv_ref[...])
        if use_rope:
            q = rope(q)
            k = rope(k)
        q = q * DN_HEAD_DIM ** -0.5
        tx = q.shape[0]
        rows = slice(off, off + tx)
        left = lax.broadcasted_iota(jnp.int32, q.shape, 1) < DN_HEAD_DIM
        k_sw = pltpu.roll(k, DN_HEAD_DIM, axis=1)
        q_sw = pltpu.roll(q, DN_HEAD_DIM, axis=1)
        vk_ref[0, 0, rows, :] = jnp.where(left, v, k_sw)
        vk_ref[0, 1, rows, :] = jnp.where(left, k_sw, v)
        qp_ref[0, 0, rows, :] = jnp.where(left, 0.0, q_sw)
        qp_ref[0, 1, rows, :] = jnp.where(left, q_sw, 0.0)
        kt_ref[0, 0, :, rows] = k_sw.T
        off += tx


def _dn_prepare(z_c, z_l, conv_w, cos, sin):
    b = z_l.shape[0]
    n_hp = DN_PAIRS

    def part(tx, off):
        return pl.BlockSpec((1, tx, LANES), lambda bi, hp: (bi, 0, off + hp))

    def wpart(off):
        return pl.BlockSpec((DN_CONV, LANES), lambda bi, hp: (0, off + hp))

    tbl = pl.BlockSpec((SEQ, LANES), lambda bi, hp: (0, 0))
    per_head = pl.BlockSpec((1, 2, DN_T, LANES), lambda bi, hp: (bi, hp, 0, 0))
    return pl.pallas_call(
        _dnprep_kernel, grid=(b, n_hp),
        in_specs=[part(CTX_LEN, 0), part(CTX_LEN, n_hp), part(CTX_LEN, 2 * n_hp),
                  part(SEQ, 0), part(SEQ, n_hp), part(SEQ, 2 * n_hp),
                  wpart(0), wpart(n_hp), wpart(2 * n_hp), tbl, tbl],
        out_specs=[per_head, per_head, pl.BlockSpec((1, 1, LANES, DN_T), lambda bi, hp: (bi, hp, 0, 0))],
        out_shape=[jax.ShapeDtypeStruct((b, DN_HEADS, DN_T, LANES), F32),
                   jax.ShapeDtypeStruct((b, DN_HEADS, DN_T, LANES), F32),
                   jax.ShapeDtypeStruct((b, n_hp, LANES, DN_T), F32)],
        compiler_params=_params(("parallel", "parallel"), VMEM_LIMIT), name="dn_prepare",
    )(z_c, z_c, z_c, z_l, z_l, z_l, conv_w, conv_w, conv_w, cos, sin)


def _gate_select_matrices():
    sel = np.zeros((DN_PAIRS, LANES, LANES), np.float32)
    for hp in range(DN_PAIRS):
        for d in range(2):
            for hh in range(2):
                src = d * DN_HEADS + 2 * hp + hh
                sel[hp, src, 2 * d + hh] = 1.0
                sel[hp, 2 * DN_HEADS + src, 4 + 2 * d + hh] = 1.0
    return jnp.asarray(sel)


def _softplus(x):
    return jnp.maximum(x, 0.0) + jnp.log1p(jnp.exp(-jnp.abs(x)))


def _gates_kernel(zc_ref, zl_ref, nega_ref, dtb_ref, sel_ref, gc_ref, gr_ref):
    z = jnp.concatenate([zc_ref[0], zl_ref[0]], axis=0)
    lane = lax.broadcasted_iota(jnp.int32, (DN_CHUNK, LANES), 1)
    ri = lax.broadcasted_iota(jnp.int32, (DN_CHUNK, DN_CHUNK), 0)
    ci = lax.broadcasted_iota(jnp.int32, (DN_CHUNK, DN_CHUNK), 1)
    lower = (ri >= ci).astype(F32)
    upper = (ri <= ci).astype(F32)
    beta = jax.nn.sigmoid(z)
    g = nega_ref[...] * _softplus(z + dtb_ref[...])
    tiles = []
    for c in range(DN_NCHUNK):
        rows = slice(c * DN_CHUNK, (c + 1) * DN_CHUNK)
        gc = g[rows]
        prefix = jnp.dot(lower, gc, precision=HIGHEST, preferred_element_type=F32)
        suffix = jnp.dot(upper, gc, precision=HIGHEST, preferred_element_type=F32)
        tiles.append(jnp.where(lane < 2 * DN_HEADS, beta[rows], jnp.where(lane < 3 * DN_HEADS, prefix, suffix)))
    tile = jnp.concatenate(tiles, axis=0)
    for hp in range(DN_PAIRS):
        cols = jnp.dot(tile, sel_ref[hp], precision=HIGHEST, preferred_element_type=F32)
        gc_ref[0, hp] = cols
        gr_ref[0, hp] = cols.T[0:8, :]


def _dn_gates(z_c, z_l, neg_a, dt_bias, sel):
    b = z_l.shape[0]
    return pl.pallas_call(
        _gates_kernel, grid=(b,),
        in_specs=[pl.BlockSpec((1, CTX_LEN, LANES), lambda bi: (bi, 0, 0)),
                  pl.BlockSpec((1, SEQ, LANES), lambda bi: (bi, 0, 0)),
                  pl.BlockSpec((1, LANES), lambda bi: (0, 0)),
                  pl.BlockSpec((1, LANES), lambda bi: (0, 0)),
                  pl.BlockSpec(sel.shape, lambda bi: (0, 0, 0))],
        out_specs=[pl.BlockSpec((1, DN_PAIRS, DN_T, LANES), lambda bi: (bi, 0, 0, 0)),
                   pl.BlockSpec((1, DN_PAIRS, 8, DN_T), lambda bi: (bi, 0, 0, 0))],
        out_shape=[jax.ShapeDtypeStruct((b, DN_PAIRS, DN_T, LANES), F32),
                   jax.ShapeDtypeStruct((b, DN_PAIRS, 8, DN_T), F32)],
        compiler_params=_params(("parallel",), VMEM_LIMIT), name="dn_gates",
    )(z_c, z_l, neg_a, dt_bias, sel)


def _dot_bf16(a, b):
    return jnp.dot(a.astype(BF16), b.astype(BF16), preferred_element_type=F32)


def _block_masks():
    ri = lax.broadcasted_iota(jnp.int32, (LANES, LANES), 0)
    ci = lax.broadcasted_iota(jnp.int32, (LANES, LANES), 1)
    same = (ri // DN_CHUNK) == (ci // DN_CHUNK)
    rin = ri % DN_CHUNK
    cin = ci % DN_CHUNK
    incl = (same & (rin >= cin), same & (rin <= cin))
    strict = (same & (rin > cin), same & (rin < cin))
    return same, incl, strict


def _scan_phase_a(p, vk_ref, qp_ref, kt_ref, gc_ref, gr_ref, lws_ref, lfin_ref, gl_ref):
    c = DN_CHUNK
    same, incl, strict = _block_masks()
    other = jnp.logical_not(same)
    rows = pl.ds(pl.multiple_of(p * LANES, LANES), LANES)
    vk_a = vk_ref[0, 0, rows, :]
    vk_b = vk_ref[0, 1, rows, :]
    qp_a = qp_ref[0, 0, rows, :]
    qp_b = qp_ref[0, 1, rows, :]
    kts = kt_ref[0, 0, :, rows]
    ktr = pltpu.roll(kts, c, axis=1)
    gct = gc_ref[0, 0, rows, :]
    grt = gr_ref[0, 0, :, rows]
    grr = pltpu.roll(grt, c, axis=1)
    lane8 = lax.broadcasted_iota(jnp.int32, (8, LANES), 1)
    left8 = lane8 < c

    items = []
    for par in range(2):
        hs = slice(par * c, (par + 1) * c)
        vks = jnp.concatenate([vk_a[hs], vk_b[hs]], axis=0)
        qq = jnp.concatenate([qp_a[hs], qp_b[hs]], axis=0)
        kk_in = jnp.where(other, vks, 0.0)
        kk = lax.dot_general(jnp.concatenate([kk_in, qq], axis=0).astype(BF16), kk_in.astype(BF16),
                             (((1,), (1,)), ((), ())), preferred_element_type=F32)
        kt_raw = (jnp.concatenate([ktr[:c], kts[c:]], axis=0) if par == 0
                  else jnp.concatenate([kts[:c], ktr[c:]], axis=0))
        for d in range(2):
            bcol = lambda col: jnp.broadcast_to(gct[hs, col:col + 1], (c, LANES))
            beta = jnp.concatenate([bcol(2 * d), bcol(2 * d + 1)], axis=0)
            g_rows = jnp.concatenate([bcol(4 + 2 * d), bcol(5 + 2 * d)], axis=0)
            if par == 0:
                g_lane = jnp.where(left8, grt[4 + 2 * d:5 + 2 * d], grr[5 + 2 * d:6 + 2 * d])
            else:
                g_lane = jnp.where(left8, grr[4 + 2 * d:5 + 2 * d], grt[5 + 2 * d:6 + 2 * d])
            last = par * c + (c - 1 if d == 0 else 0)
            gl_a = jnp.broadcast_to(grt[4 + 2 * d:5 + 2 * d, last:last + 1], (8, LANES))
            gl_b = jnp.broadcast_to(grt[5 + 2 * d:6 + 2 * d, last:last + 1], (8, LANES))
            g_last = jnp.where(left8, gl_a, gl_b)
            g_cols = jnp.broadcast_to(g_lane[0:1], (LANES, LANES))
            decay = jnp.where(incl[d], jnp.exp(jnp.where(incl[d], g_rows - g_cols, 0.0)), 0.0)
            a = jnp.where(strict[d], kk[:LANES] * decay, 0.0) * beta
            qk = kk[LANES:] * decay
            eg = jnp.exp(g_rows)
            x = vks * beta * jnp.where(other, eg, 1.0)
            qe = qq * eg
            kt_s = jnp.where(other, kt_raw * jnp.exp(jnp.broadcast_to(g_last[0:1] - g_lane[0:1], (LANES, LANES))),
                             0.0)
            idx = (2 * p + par) * 2 + d
            lws_ref[idx, LANES:, :] = qe.astype(BF16)
            lfin_ref[idx] = jnp.concatenate([qk, kt_s], axis=0).astype(BF16)
            gl_ref[idx] = jnp.exp(g_last)
            items.append(dict(idx=idx, d=d, a=a, x=x))
    return items


def _scan_solve(items, xs_ref, lws_ref):
    ri = lax.broadcasted_iota(jnp.int32, (LANES, LANES), 0)
    ci = lax.broadcasted_iota(jnp.int32, (LANES, LANES), 1)
    eye = (ri == ci).astype(F32)
    base = DN_INV_BASE

    def off_block(size, d):
        row_second = (ri // size) % 2 == 1
        col_second = (ci // size) % 2 == 1
        inside = (ri // (2 * size)) == (ci // (2 * size))
        return inside & ((row_second & ~col_second) if d == 0 else (~row_second & col_second))

    for it in items:
        it["a8"] = jnp.where((ri // base) == (ci // base), it["a"], 0.0)
        it["t"] = eye - it["a8"]
    for it in items:
        it["p"] = _dot_bf16(it["a8"], it["a8"])
    for it in items:
        r = _dot_bf16(it["p"], jnp.concatenate([it["p"], it["t"]], axis=1))
        it["p"] = r[:, :LANES]
        it["t"] = it["t"] + r[:, LANES:]
    for it in items:
        it["t"] = it["t"] + _dot_bf16(it["p"], it["t"])
    size = base
    while size < DN_CHUNK:
        for it in items:
            it["m"] = _dot_bf16(jnp.where(off_block(size, it["d"]), it["a"], 0.0), it["t"])
        for it in items:
            it["t"] = it["t"] - _dot_bf16(it["t"], it["m"])
        size *= 2
    for it in items:
        it["x"] = _dot_bf16(it["t"], it["x"])
    for it in items:
        xs_ref[it["idx"]] = it["x"]
        lws_ref[it["idx"], :LANES, :] = it["x"].astype(BF16)


def _scan_phase_b(lo, hi, states, xs_ref, lws_ref, lfin_ref, gl_ref, out_refs):
    c = DN_CHUNK
    same, _, _ = _block_masks()
    other = jnp.logical_not(same)
    left = lax.broadcasted_iota(jnp.int32, (c, LANES), 1) < c

    def body(s, carry):
        u_f = s
        u_b = jnp.where(s < DN_CTX_CHUNKS, DN_CTX_CHUNKS - 1 - s, DN_NCHUNK + DN_CTX_CHUNKS - 1 - s)
        us = (u_f, u_b)
        idx = [us[d] * 2 + d for d in range(2)]
        res = [jnp.dot(lws_ref[idx[d]], carry[d].astype(BF16), preferred_element_type=F32) for d in range(2)]
        vn = [(xs_ref[idx[d]] - res[d][:LANES]).astype(BF16) for d in range(2)]
        fin = [jnp.dot(lfin_ref[idx[d]], vn[d], preferred_element_type=F32) for d in range(2)]
        new = []
        for d in range(2):
            if out_refs is not None:
                o = jnp.where(left, fin[d][:c] + res[d][LANES:LANES + c],
                              fin[d][c:LANES] + res[d][LANES + c:])
                out_refs[d][0, pl.ds(pl.multiple_of((us[d] - DN_CTX_CHUNKS) * c, c), c), :] = o
            decay = jnp.broadcast_to(gl_ref[idx[d]][0:1], (LANES, LANES))
            new.append(carry[d] * decay + jnp.where(other, fin[d][LANES:], 0.0))
        return tuple(new)

    return lax.fori_loop(lo, hi, body, states)


def _scan_kernel(vk_ref, qp_ref, kt_ref, gc_ref, gr_ref, of_ref, ob_ref, xs_ref, lws_ref, lfin_ref, gl_ref):
    def phase_a(step, carry):
        items = []
        for j in range(DN_PAIRS_PER_STEP):
            items += _scan_phase_a(step * DN_PAIRS_PER_STEP + j, vk_ref, qp_ref, kt_ref, gc_ref, gr_ref,
                                   lws_ref, lfin_ref, gl_ref)
        _scan_solve(items, xs_ref, lws_ref)
        return carry

    lax.fori_loop(0, DN_T // LANES // DN_PAIRS_PER_STEP, phase_a, 0)
    zero = jnp.zeros((LANES, LANES), F32)
    scratch = (xs_ref, lws_ref, lfin_ref, gl_ref)
    st = _scan_phase_b(0, DN_CTX_CHUNKS, (zero, zero), *scratch, None)
    _scan_phase_b(DN_CTX_CHUNKS, DN_NCHUNK, st, *scratch, (of_ref, ob_ref))


def _dn_scan(vk, qp, kt, g_cols, g_rows):
    b = vk.shape[0]
    n_items = 2 * DN_NCHUNK
    out = pl.BlockSpec((1, SEQ, LANES), lambda bi, hp: (bi, 0, hp))
    return pl.pallas_call(
        _scan_kernel, grid=(b, DN_PAIRS),
        in_specs=[pl.BlockSpec((1, 2, DN_T, LANES), lambda bi, hp: (bi, hp, 0, 0)),
                  pl.BlockSpec((1, 2, DN_T, LANES), lambda bi, hp: (bi, hp, 0, 0)),
                  pl.BlockSpec((1, 1, LANES, DN_T), lambda bi, hp: (bi, hp, 0, 0)),
                  pl.BlockSpec((1, 1, DN_T, LANES), lambda bi, hp: (bi, hp, 0, 0)),
                  pl.BlockSpec((1, 1, 8, DN_T), lambda bi, hp: (bi, hp, 0, 0))],
        out_specs=[out, out],
        out_shape=[jax.ShapeDtypeStruct((b, SEQ, DN_WIDTH), F32)] * 2,
        scratch_shapes=[pltpu.VMEM((n_items, LANES, LANES), F32),
                        pltpu.VMEM((n_items, 2 * LANES, LANES), BF16),
                        pltpu.VMEM((n_items, 2 * LANES, LANES), BF16),
                        pltpu.VMEM((n_items, 8, LANES), F32)],
        compiler_params=_params(("parallel", "parallel"), VMEM_LIMIT), name="dn_scan",
    )(vk, qp, kt, g_cols, g_rows)


POST_TM = 256
ROW_TILE = 8


def _store_row_tiles(ref, val):
    m = val.shape[0]
    for j in range(ROW_TILE):
        ref[pl.ds(j, m, stride=ROW_TILE), :] = val[:, j * LANES:(j + 1) * LANES]


def _load_row_tiles(ref, first, m):
    return jnp.concatenate([ref[pl.ds(first * ROW_TILE + j, m, stride=ROW_TILE), :] for j in range(ROW_TILE)],
                           axis=1)


def _post_kernel(of_ref, ob_ref, dg_ref, ona_ref, gab_ref, x_ref, g1_ref, sh2_ref, sc2_ref,
                 dnw_ref, wa_ref, wb_ref, wo_ref, n2w_ref, wr_ref, br_ref,
                 xl_ref, h2_ref, route_ref, cnt_ref):
    i = pl.program_id(0)

    @pl.when(i == 0)
    def _():
        cnt_ref[...] = jnp.zeros_like(cnt_ref)

    o = of_ref[...] + ob_ref[...]
    hi = lax.broadcasted_iota(jnp.int32, (DN_WIDTH, DN_WIDTH), 0) // DN_HEAD_DIM
    hj = lax.broadcasted_iota(jnp.int32, (DN_WIDTH, DN_WIDTH), 1) // DN_HEAD_DIM
    head_mean = jnp.where(hi == hj, 1.0 / DN_HEAD_DIM, 0.0).astype(BF16)
    ms = _dot_split_lhs(o * o, head_mean)
    dg = dg_ref[...]
    o_dn = (o * lax.rsqrt(ms + NORM_EPS) * dnw_ref[...]) * (dg * jax.nn.sigmoid(dg))
    gab = gab_ref[...]
    ya = jnp.dot(ona_ref[...], wa_ref[...], preferred_element_type=F32)
    yb = jnp.dot(o_dn.astype(BF16), wb_ref[...], preferred_element_type=F32)
    y = jax.nn.sigmoid(gab[:, :D_MODEL]) * ya + jax.nn.sigmoid(gab[:, D_MODEL:]) * yb
    y = jnp.dot(y.astype(BF16), wo_ref[...], preferred_element_type=F32)
    xl = x_ref[...] + g1_ref[0] * y
    xl_ref[...] = xl
    ms2 = jnp.mean(xl * xl, axis=-1, keepdims=True)
    h2 = (xl * lax.rsqrt(ms2 + NORM_EPS) * n2w_ref[...]) * (1.0 + sc2_ref[0]) + sh2_ref[0]
    _store_row_tiles(h2_ref, h2)

    h_hi, h_lo = _split_bf16(h2)
    logits = (jnp.dot(h_hi, wr_ref[0], preferred_element_type=F32) + jnp.dot(h_lo, wr_ref[0], preferred_element_type=F32)
              + jnp.dot(h_hi, wr_ref[1], preferred_element_type=F32) + br_ref[...])
    tm = logits.shape[0]
    lane = lax.broadcasted_iota(jnp.int32, (tm, LANES), 1).astype(F32)
    vals, idxs = [], []
    cur = logits
    for _ in range(TOP_K):
        m = jnp.max(cur, axis=-1, keepdims=True)
        idx = jnp.min(jnp.where(cur == m, lane, float(LANES)), axis=-1, keepdims=True)
        vals.append(m)
        idxs.append(idx)
        cur = jnp.where(lane == idx, -jnp.inf, cur)
    es = [jnp.exp(v - vals[0]) for v in vals]
    den = es[0] + es[1] + es[2] + es[3]
    onehot = jnp.zeros((tm, LANES), F32)
    for idx in idxs:
        onehot = onehot + jnp.where(lane == idx, 1.0, 0.0)
    ti = lax.broadcasted_iota(jnp.int32, (tm, tm), 0)
    tj = lax.broadcasted_iota(jnp.int32, (tm, tm), 1)
    before = jnp.where(ti > tj, 1.0, 0.0).astype(BF16)
    cnt = cnt_ref[...] + jnp.dot(before, onehot.astype(BF16), preferred_element_type=F32)
    route = jnp.zeros((tm, LANES), F32)
    for kk in range(TOP_K):
        rank = jnp.sum(jnp.where(lane == idxs[kk], cnt, 0.0), axis=-1, keepdims=True)
        route = jnp.where(lane == float(kk), es[kk] / den, route)
        route = jnp.where(lane == float(TOP_K + kk), idxs[kk], route)
        route = jnp.where(lane == float(2 * TOP_K + kk), rank, route)
    route_ref[...] = route
    cnt_ref[...] = cnt_ref[...] + jnp.sum(onehot, axis=0, keepdims=True)


def _post_mixer(o_f, o_b, dg, o_na, gab, x, g1, sh2, sc2, dn_norm_w, w_br_a, w_br_b, w_out, norm2_w, wr, br):
    n, d = x.shape
    tm = POST_TM
    per_batch = SEQ // tm
    tok = lambda w: pl.BlockSpec((tm, w), lambda i: (i, 0))
    row = pl.BlockSpec((1, 1, d), lambda i: (i // per_batch, 0, 0))
    const = lambda a: pl.BlockSpec(a.shape, lambda i: (0,) * a.ndim)
    return pl.pallas_call(
        _post_kernel, grid=(n // tm,),
        in_specs=[tok(DN_WIDTH), tok(DN_WIDTH), tok(DN_WIDTH), tok(NA_WIDTH), tok(2 * D_MODEL), tok(d),
                  row, row, row, const(dn_norm_w), const(w_br_a), const(w_br_b), const(w_out),
                  const(norm2_w), const(wr), const(br)],
        out_specs=[tok(d), pl.BlockSpec((tm * ROW_TILE, LANES), lambda i: (i, 0)), tok(LANES),
                   pl.BlockSpec((1, LANES), lambda i: (0, 0))],
        out_shape=[jax.ShapeDtypeStruct((n, d), F32), jax.ShapeDtypeStruct((n * ROW_TILE, LANES), F32),
                   jax.ShapeDtypeStruct((n, LANES), F32), jax.ShapeDtypeStruct((1, LANES), F32)],
        compiler_params=_params(("arbitrary",), VMEM_LIMIT), name="post_mixer_router",
    )(o_f, o_b, dg, o_na, gab, x, g1, sh2, sc2, dn_norm_w, w_br_a, w_br_b, w_out, norm2_w, wr, br)


SLOT_CHUNK = 8192
SLOT_UNROLL = 16


def _slot_map_kernel(lo_ref, hi_ref, dest_ref, tok_ref):
    c = pl.program_id(0)

    @pl.when(c == 0)
    def _():
        def fill_range(r, carry):
            def fill(s, inner):
                tok_ref[s] = 0
                return inner
            return lax.fori_loop(lo_ref[r], hi_ref[r], fill, carry)
        lax.fori_loop(0, N_EXPERTS + 1, fill_range, 0)

    def put(i, carry):
        slots = [dest_ref[0, 0, i * SLOT_UNROLL + j] for j in range(SLOT_UNROLL)]
        first_tok = lax.shift_right_logical(c * SLOT_CHUNK + i * SLOT_UNROLL, TOP_K.bit_length() - 1)
        for j in range(SLOT_UNROLL):
            tok_ref[slots[j]] = first_tok + j // TOP_K
        return carry
    lax.fori_loop(0, SLOT_CHUNK // SLOT_UNROLL, put, 0)


def _slot_map(pad_lo, pad_hi, dest):
    grid_spec = pltpu.PrefetchScalarGridSpec(
        num_scalar_prefetch=2, grid=(N_ASG // SLOT_CHUNK,),
        in_specs=[pl.BlockSpec((1, 1, SLOT_CHUNK), lambda c, lo, hi: (c, 0, 0), memory_space=pltpu.SMEM)],
        out_specs=pl.BlockSpec((MOE_MPAD,), lambda c, lo, hi: (0,), memory_space=pltpu.SMEM))
    return pl.pallas_call(
        _slot_map_kernel, grid_spec=grid_spec,
        out_shape=jax.ShapeDtypeStruct((MOE_MPAD,), jnp.int32),
        compiler_params=_params(("arbitrary",)), name="slot_map",
    )(pad_lo, pad_hi, dest.reshape(N_ASG // SLOT_CHUNK, 1, SLOT_CHUNK))


MOE_LOOKAHEAD = 2
MOE_SLOTS = MOE_LOOKAHEAD + 1
DMA_ISSUE_UNROLL = 8
DMA_WAIT_UNROLL = 32


def _row_copy(src_hbm, dst_vmem, src_row, dst_row, sem):
    src = pl.ds(pl.multiple_of(src_row * ROW_TILE, ROW_TILE), ROW_TILE)
    dst = pl.ds(pl.multiple_of(dst_row * ROW_TILE, ROW_TILE), ROW_TILE)
    return pltpu.make_async_copy(src_hbm.at[src], dst_vmem.at[dst], sem)


def _expert_kernel(blk_e_ref, nact_ref, tok_ref, par_ref, nxt_ref, h_hbm, w1_hbm, b1_ref, w2_hbm, b2_ref, y_ref,
                   xbuf, w1s, w2s, w1b, w2b, sems, wsems):
    i = pl.program_id(0)
    n_active = nact_ref[0]
    bm = MOE_BM

    def issue(blk, slot):
        def body(g, carry):
            for j in range(DMA_ISSUE_UNROLL):
                s = g * DMA_ISSUE_UNROLL + j
                _row_copy(h_hbm, xbuf.at[slot], tok_ref[blk * bm + s], s, sems.at[slot]).start()
            return carry
        lax.fori_loop(0, bm // DMA_ISSUE_UNROLL, body, 0)

    def wait(slot):
        def body(g, carry):
            for j in range(DMA_WAIT_UNROLL):
                _row_copy(h_hbm, xbuf.at[slot], 0, g * DMA_WAIT_UNROLL + j, sems.at[slot]).wait()
            return carry
        lax.fori_loop(0, bm // DMA_WAIT_UNROLL, body, 0)

    def weight_copies(e, p):
        return (pltpu.make_async_copy(w1_hbm.at[e], w1s.at[p], wsems.at[p]),
                pltpu.make_async_copy(w2_hbm.at[e], w2s.at[p], wsems.at[p]))

    @pl.when(i == 0)
    def _():
        for cp in weight_copies(blk_e_ref[0], par_ref[0]):
            cp.start()
        for j in range(MOE_LOOKAHEAD):
            issue(jnp.minimum(j, n_active - 1), j)

    changed = jnp.logical_or(i == 0, blk_e_ref[i] != blk_e_ref[jnp.maximum(i - 1, 0)])

    @pl.when(jnp.logical_and(i < n_active, changed))
    def _():
        p = par_ref[i]
        for cp in weight_copies(blk_e_ref[i], p):
            cp.wait()
        w1b[...] = w1s[p].astype(BF16)
        w2b[...] = w2s[p].astype(BF16)

        @pl.when(nxt_ref[i] >= 0)
        def _():
            for cp in weight_copies(nxt_ref[i], 1 - p):
                cp.start()

    @pl.when(i < n_active)
    def _():
        slot = i % MOE_SLOTS
        wait(slot)
        xb = _load_row_tiles(xbuf.at[slot], 0, bm).astype(BF16)
        nxt = jnp.minimum(i + MOE_LOOKAHEAD, n_active - 1)
        nslot = (i + MOE_LOOKAHEAD) % MOE_SLOTS
        for s in range(bm):
            _row_copy(h_hbm, xbuf.at[nslot], tok_ref[nxt * bm + s], s, sems.at[nslot]).start()
        hb = jnp.dot(xb, w1b[...], preferred_element_type=F32) + b1_ref[0]
        gate = jnp.minimum(hb[:, :D_EXPERT], SWIGLU_LIMIT)
        up = jnp.clip(hb[:, D_EXPERT:], -SWIGLU_LIMIT, SWIGLU_LIMIT)
        act = (up + 1.0) * gate * jax.nn.sigmoid(SWIGLU_ALPHA * gate)
        _store_row_tiles(y_ref, jnp.dot(act.astype(BF16), w2b[...], preferred_element_type=F32) + b2_ref[0])

    @pl.when(i == n_active)
    def _():
        for j in range(MOE_LOOKAHEAD):
            wait((i + j) % MOE_SLOTS)

    @pl.when(i >= n_active)
    def _():
        y_ref[...] = jnp.zeros_like(y_ref)


def _expert_ffn(blk_expert, n_active, tok_pad, stage_slot, next_expert, h2, w1, b1, w2, b2):
    d = w1.shape[1]
    f2 = w1.shape[2]

    def live(i, nact):
        return jnp.minimum(i, jnp.maximum(nact[0] - 1, 0))

    bias = lambda w: pl.BlockSpec((1, 1, w), lambda i, be, na, tk, sp, ne: (be[live(i, na)], 0, 0))
    grid_spec = pltpu.PrefetchScalarGridSpec(
        num_scalar_prefetch=5, grid=(MOE_NBLK,),
        in_specs=[pl.BlockSpec(memory_space=pl.ANY), pl.BlockSpec(memory_space=pl.ANY), bias(f2),
                  pl.BlockSpec(memory_space=pl.ANY), bias(d)],
        out_specs=pl.BlockSpec((MOE_BM * ROW_TILE, LANES), lambda i, be, na, tk, sp, ne: (i, 0)),
        scratch_shapes=[pltpu.VMEM((MOE_SLOTS, MOE_BM * ROW_TILE, LANES), F32),
                        pltpu.VMEM((2, d, f2), F32), pltpu.VMEM((2, f2 // 2, d), F32),
                        pltpu.VMEM((d, f2), BF16), pltpu.VMEM((f2 // 2, d), BF16),
                        pltpu.SemaphoreType.DMA((MOE_SLOTS,)), pltpu.SemaphoreType.DMA((2,))])
    return pl.pallas_call(
        _expert_kernel, grid_spec=grid_spec,
        out_shape=jax.ShapeDtypeStruct((MOE_MPAD * ROW_TILE, LANES), F32),
        compiler_params=_params(("arbitrary",), VMEM_LIMIT), name="expert_ffn",
    )(blk_expert, n_active, tok_pad, stage_slot, next_expert, h2, w1, b1, w2, b2)


COMB_TM = 128


def _combine_kernel(dest_ref, y_hbm, xl_ref, route_ref, g2_ref, fw_ref, o_ref, ybuf, sems):
    i = pl.program_id(0)
    n_steps = pl.num_programs(0)
    tm = COMB_TM

    def issue(step, slot):
        def body(g, carry):
            for j in range(DMA_ISSUE_UNROLL // TOP_K):
                t = g * (DMA_ISSUE_UNROLL // TOP_K) + j
                for kk in range(TOP_K):
                    _row_copy(y_hbm, ybuf.at[slot], dest_ref[(step * tm + t) * TOP_K + kk], kk * tm + t,
                              sems.at[slot]).start()
            return carry
        lax.fori_loop(0, tm * TOP_K // DMA_ISSUE_UNROLL, body, 0)

    def wait(slot):
        def body(g, carry):
            for j in range(DMA_WAIT_UNROLL):
                _row_copy(y_hbm, ybuf.at[slot], 0, g * DMA_WAIT_UNROLL + j, sems.at[slot]).wait()
            return carry
        lax.fori_loop(0, tm * TOP_K // DMA_WAIT_UNROLL, body, 0)

    @pl.when(i == 0)
    def _():
        issue(0, 0)

    @pl.when(i + 1 < n_steps)
    def _():
        issue(i + 1, (i + 1) % 2)

    slot = i % 2
    wait(slot)
    route = route_ref[...]
    moe = jnp.zeros((tm, D_MODEL), F32)
    for kk in range(TOP_K):
        moe = moe + route[:, kk:kk + 1] * _load_row_tiles(ybuf.at[slot], kk * tm, tm)
    xo = xl_ref[...] + g2_ref[0] * moe
    ms = jnp.mean(xo * xo, axis=-1, keepdims=True)
    o_ref[...] = xo * lax.rsqrt(ms + NORM_EPS) * fw_ref[...]


def _combine(dest, y_sorted, xl, route, g2, final_w):
    n, d = xl.shape
    tm = COMB_TM
    per_batch = SEQ // tm
    grid_spec = pltpu.PrefetchScalarGridSpec(
        num_scalar_prefetch=1, grid=(n // tm,),
        in_specs=[pl.BlockSpec(memory_space=pl.ANY),
                  pl.BlockSpec((tm, d), lambda i, ds: (i, 0)),
                  pl.BlockSpec((tm, LANES), lambda i, ds: (i, 0)),
                  pl.BlockSpec((1, 1, d), lambda i, ds: (i // per_batch, 0, 0)),
                  pl.BlockSpec((1, d), lambda i, ds: (0, 0))],
        out_specs=pl.BlockSpec((tm, d), lambda i, ds: (i, 0)),
        scratch_shapes=[pltpu.VMEM((2, TOP_K * tm * ROW_TILE, LANES), F32), pltpu.SemaphoreType.DMA((2,))])
    return pl.pallas_call(
        _combine_kernel, grid_spec=grid_spec,
        out_shape=jax.ShapeDtypeStruct((n, d), F32),
        compiler_params=_params(("arbitrary",), VMEM_LIMIT), name="moe_combine",
    )(dest, y_sorted, xl, route, g2, final_w)


def _split_in_weights(w_in):
    o = np.cumsum((0, NA_WIDTH, NA_WIDTH, NA_WIDTH, 3 * DN_WIDTH, DN_WIDTH, 2 * DN_HEADS, 2 * DN_HEADS,
                   D_MODEL, D_MODEL))
    wb = w_in.astype(BF16)
    w_na = wb[:, o[0]:o[3]]
    w_dq = wb[:, o[3]:o[4]]
    w_dg = wb[:, o[4]:o[5]]
    w_gab = wb[:, o[7]:o[9]]
    w_dba = jnp.pad(wb[:, o[5]:o[7]], ((0, 0), (0, LANES - 4 * DN_HEADS)))
    return w_na, w_dq, w_dg, w_gab, w_dba


def kernel(x, c, ctx, c_ctx, w_mod, b_mod, norm1_w, w_in, na_rpb, dn_conv_w, dn_a_log, dn_dt_bias, dn_norm_w,
           w_br_a, w_br_b, w_out, norm2_w, w_router, b_router, w1, b1, w2, b2, final_norm_w):
    d = D_MODEL
    cc = jnp.concatenate([c, c_ctx[None], jnp.zeros((16 - BATCH - 1, d), F32)], axis=0)
    mod = _modulation(cc, w_mod[0], b_mod[0][None])
    mod_l = mod[:BATCH].reshape(BATCH, 6, 1, d)
    sh1, sc1, g1, sh2, sc2, g2 = (mod_l[:, i] for i in range(6))
    mod_c = jnp.broadcast_to(mod[BATCH].reshape(6, 1, 1, d), (6, BATCH, 1, d))

    weights = _split_in_weights(w_in[0])
    n1w = norm1_w[0][None]
    naq, nak, nav, dqkv_l, dg_l, gab_l, dba_l = _in_projection(x, sh1, sc1, n1w, weights, 512)
    _, nak_c, nav_c, dqkv_c, _, _, dba_c = _in_projection(ctx, mod_c[0], mod_c[1], n1w, weights, CTX_LEN)

    o_na = _neighborhood_attention(naq, nak, nav, nak_c, nav_c, _na_bias_table(na_rpb[0]))

    cos, sin = _rope_tables(SEQ)
    vk, qp, kt = _dn_prepare(dqkv_c, dqkv_l, dn_conv_w[0], cos, sin)
    pad16 = lambda v: jnp.pad(v.reshape(1, 2 * DN_HEADS), ((0, 0), (2 * DN_HEADS, LANES - 4 * DN_HEADS)))
    g_cols, g_rows = _dn_gates(dba_c, dba_l, pad16(-jnp.exp(dn_a_log[0])), pad16(dn_dt_bias[0]),
                               _gate_select_matrices())
    o_f, o_b = _dn_scan(vk, qp, kt, g_cols, g_rows)

    flat = lambda a: a.reshape(N_TOK, a.shape[-1])
    wr = jnp.stack(_split_bf16(jnp.pad(w_router[0], ((0, 0), (0, LANES - N_EXPERTS)))))
    br = jnp.concatenate([b_router[0], jnp.full((LANES - N_EXPERTS,), NEG_INF, F32)])[None]
    dnw = jnp.tile(dn_norm_w[0], DN_HEADS)[None]
    xl, h2, route, counts = _post_mixer(
        flat(o_f), flat(o_b), flat(dg_l), flat(o_na), flat(gab_l), flat(x), g1, sh2, sc2, dnw,
        w_br_a[0].astype(BF16), w_br_b[0].astype(BF16), w_out[0].astype(BF16), norm2_w[0][None], wr, br)

    e_idx = route[:, TOP_K:2 * TOP_K].astype(jnp.int32)
    rank = route[:, 2 * TOP_K:3 * TOP_K].astype(jnp.int32)
    cnt = counts[0, :N_EXPERTS].astype(jnp.int32)
    padded = (cnt + MOE_BM - 1) // MOE_BM * MOE_BM
    pad_ends = jnp.cumsum(padded)
    pad_starts = pad_ends - padded
    dest = (pad_starts[e_idx] + rank).reshape(-1)
    total = jnp.full((1,), MOE_MPAD, jnp.int32)
    tok_pad = _slot_map(jnp.concatenate([pad_starts + cnt, pad_ends[-1:]]), jnp.concatenate([pad_ends, total]), dest)
    blk_first_row = jnp.arange(MOE_NBLK, dtype=jnp.int32) * MOE_BM
    blk_expert = jnp.minimum(jnp.sum((pad_ends[None, :] <= blk_first_row[:, None]).astype(jnp.int32), axis=1),
                             N_EXPERTS - 1)
    n_active = (pad_ends[-1:] // MOE_BM).astype(jnp.int32)

    new_group = jnp.concatenate([jnp.ones((1,), jnp.int32), (blk_expert[1:] != blk_expert[:-1]).astype(jnp.int32)])
    stage_slot = jnp.cumsum(new_group) % 2
    ids = jnp.arange(N_EXPERTS, dtype=jnp.int32)
    later = (ids[None, :] > ids[:, None]) & (cnt[None, :] > 0)
    next_nonempty = jnp.min(jnp.where(later, ids[None, :], N_EXPERTS), axis=1)
    next_expert = jnp.where(next_nonempty < N_EXPERTS, next_nonempty, -1)[blk_expert]

    y_sorted = _expert_ffn(blk_expert, n_active, tok_pad, stage_slot, next_expert, h2, w1[0], b1[0][:, None], w2[0],
                           b2[0][:, None])
    out = _combine(dest, y_sorted, xl, route, g2, final_norm_w[None])
    return out.reshape(BATCH, SEQ, d)
```

```python
import functools

import numpy as np
import jax
import jax.numpy as jnp
from jax import lax
from jax.experimental import pallas as pl
from jax.experimental.pallas import tpu as pltpu

F32 = jnp.float32
BF16 = jnp.bfloat16
HIGHEST = lax.Precision.HIGHEST

D_MODEL = 1024
BATCH = 8
SEQ = 2048
GRID_W = 64
GRID_ROWS = SEQ // GRID_W
CTX_LEN = 256
NA_HEADS = 8
NA_HEAD_DIM = 64
NA_KH = 8
NA_KW = 16
NA_SCALE = NA_HEAD_DIM ** -0.5
NA_WIDTH = NA_HEADS * NA_HEAD_DIM
DN_HEADS = 8
DN_HEAD_DIM = 64
DN_WIDTH = DN_HEADS * DN_HEAD_DIM
DN_CONV = 5
DN_CHUNK = 64
ROPE_THETA = 10000.0
N_EXPERTS = 32
TOP_K = 4
D_EXPERT = 1024
SWIGLU_LIMIT = 7.0
SWIGLU_ALPHA = 1.702
NORM_EPS = 1e-6
NEG_INF = -1e30

LANES = 128
N_TOK = BATCH * SEQ
N_ASG = N_TOK * TOP_K
MOE_BM = 256
MOE_MPAD = -(-(N_ASG + N_EXPERTS * (MOE_BM - 1)) // MOE_BM) * MOE_BM
MOE_NBLK = MOE_MPAD // MOE_BM
VMEM_LIMIT = 56 * 1024 * 1024

DN_T = CTX_LEN + SEQ
DN_NCHUNK = DN_T // DN_CHUNK
DN_CTX_CHUNKS = CTX_LEN // DN_CHUNK
DN_PAIRS = DN_HEADS // 2
DN_PAIRS_PER_STEP = 6
DN_INV_BASE = 8


def _params(sem, vmem=None):
    return pltpu.CompilerParams(dimension_semantics=sem, vmem_limit_bytes=vmem)


def _split_bf16(x):
    hi = x.astype(BF16)
    return hi, (x - hi.astype(F32)).astype(BF16)


def _dot_split_lhs(x, w_bf16):
    hi, lo = _split_bf16(x)
    return (jnp.dot(hi, w_bf16, preferred_element_type=F32) + jnp.dot(lo, w_bf16, preferred_element_type=F32))


def _mod_kernel(c_ref, w_ref, b_ref, o_ref):
    c = c_ref[...]
    s = c * jax.nn.sigmoid(c)
    o_ref[...] = jnp.dot(s, w_ref[...], precision=HIGHEST, preferred_element_type=F32) + b_ref[...]


def _modulation(cc, w_mod, b_mod):
    rows, d = cc.shape
    n = w_mod.shape[1]
    bn = 1024
    return pl.pallas_call(
        _mod_kernel, grid=(n // bn,),
        in_specs=[pl.BlockSpec((rows, d), lambda j: (0, 0)),
                  pl.BlockSpec((d, bn), lambda j: (0, j)),
                  pl.BlockSpec((1, bn), lambda j: (0, j))],
        out_specs=pl.BlockSpec((rows, bn), lambda j: (0, j)),
        out_shape=jax.ShapeDtypeStruct((rows, n), F32),
        compiler_params=_params(("parallel",)), name="modulation")(cc, w_mod, b_mod)


def _inproj_kernel(x_ref, sh_ref, sc_ref, nw_ref, wna_ref, wdq_ref, wdg_ref, wgab_ref, wdba_ref,
                   q_ref, k_ref, v_ref, dq_ref, dg_ref, gab_ref, dba_ref):
    x = x_ref[0]
    ms = jnp.mean(x * x, axis=-1, keepdims=True)
    y = x * lax.rsqrt(ms + NORM_EPS) * nw_ref[...]
    h = (y * (1.0 + sc_ref[0]) + sh_ref[0]).astype(BF16)
    na = jnp.dot(h, wna_ref[...], preferred_element_type=F32)
    q_ref[0] = (na[:, :NA_WIDTH] * NA_SCALE).astype(BF16)
    k_ref[0] = na[:, NA_WIDTH:2 * NA_WIDTH].astype(BF16)
    v_ref[0] = na[:, 2 * NA_WIDTH:].astype(BF16)
    dq_ref[0] = jnp.dot(h, wdq_ref[...], preferred_element_type=F32)
    dg_ref[0] = jnp.dot(h, wdg_ref[...], preferred_element_type=F32)
    gab_ref[0] = jnp.dot(h, wgab_ref[...], preferred_element_type=F32)
    dba_ref[0] = jnp.dot(h, wdba_ref[...], preferred_element_type=F32)


def _in_projection(x, shift, scale, norm_w, weights, tm):
    bx, tx, d = x.shape
    wna, wdq, wdg, wgab, wdba = weights
    tok = lambda w: pl.BlockSpec((1, tm, w), lambda b, i: (b, i, 0))
    row = pl.BlockSpec((1, 1, d), lambda b, i: (b, 0, 0))
    const = lambda a: pl.BlockSpec(a.shape, lambda b, i: (0,) * a.ndim)
    widths = (NA_WIDTH, NA_WIDTH, NA_WIDTH, 3 * DN_WIDTH, DN_WIDTH, 2 * D_MODEL, LANES)
    dtypes = (BF16, BF16, BF16, F32, F32, F32, F32)
    return pl.pallas_call(
        _inproj_kernel, grid=(bx, tx // tm),
        in_specs=[tok(d), row, row, const(norm_w), const(wna), const(wdq), const(wdg), const(wgab), const(wdba)],
        out_specs=[tok(w) for w in widths],
        out_shape=[jax.ShapeDtypeStruct((bx, tx, w), dt) for w, dt in zip(widths, dtypes)],
        compiler_params=_params(("parallel", "parallel"), VMEM_LIMIT), name="in_projection",
    )(x, shift, scale, norm_w, wna, wdq, wdg, wgab, wdba)


def _na_bias_table(rpb):
    col = np.arange(GRID_W)
    col_start = np.clip(col - NA_KW // 2, 0, GRID_W - NA_KW)
    mask = (col[None, :] >= col_start[:, None]) & (col[None, :] < col_start[:, None] + NA_KW)
    dc = np.clip(col[None, :] - col[:, None] + NA_KW - 1, 0, 2 * NA_KW - 2)
    pick = np.zeros((2 * NA_KW - 1, GRID_W * GRID_W), np.float32)
    pick[dc.reshape(-1), np.arange(GRID_W * GRID_W)] = 1.0
    n_dr = 2 * NA_KH - 1
    by_dr = jnp.dot(rpb.astype(F32).reshape(NA_HEADS * n_dr, 2 * NA_KW - 1), pick, precision=HIGHEST)
    by_dr = jnp.where(mask[None, None], by_dr.reshape(NA_HEADS, n_dr, GRID_W, GRID_W), NEG_INF)
    tbl = jnp.stack([by_dr[:, c:c + NA_KH] for c in range(NA_KH)], axis=0)
    return tbl.transpose(0, 1, 3, 2, 4).reshape(NA_KH, NA_HEADS, GRID_W, NA_KH * GRID_W)


NA_HEAD_GROUP = 8


def _na_first_row(r):
    return jnp.clip(r - NA_KH // 2, 0, GRID_ROWS - NA_KH)


def _na_kernel(q_ref, k_ref, v_ref, kc_ref, vc_ref, bias_ref, o_ref):
    r = pl.program_id(1)
    start = pl.multiple_of(_na_first_row(r) * GRID_W, GRID_W)
    q = q_ref[0]
    kw = k_ref[0, pl.ds(start, NA_KH * GRID_W), :]
    vw = v_ref[0, pl.ds(start, NA_KH * GRID_W), :]
    kc = kc_ref[0]
    vc = vc_ref[0]
    nt = (((1,), (1,)), ((), ()))
    outs = []
    first_head = lax.broadcasted_iota(jnp.int32, (GRID_W, LANES), 1) < NA_HEAD_DIM
    zero = jnp.zeros((), BF16)
    for g0 in range(0, NA_HEADS, NA_HEAD_GROUP):
        heads = range(g0, g0 + NA_HEAD_GROUP)
        blk = [slice((h // 2) * LANES, (h // 2 + 1) * LANES) for h in heads]
        qm = [jnp.where(first_head if h % 2 == 0 else ~first_head, q[:, sl], zero) for h, sl in zip(heads, blk)]
        s = [lax.dot_general(a, kw[:, sl], nt, preferred_element_type=F32) + bias_ref[0, h]
             for a, sl, h in zip(qm, blk, heads)]
        sc = [lax.dot_general(a, kc[:, sl], nt, preferred_element_type=F32) for a, sl in zip(qm, blk)]
        m = [jnp.maximum(jnp.max(a, axis=-1, keepdims=True), jnp.max(b, axis=-1, keepdims=True))
             for a, b in zip(s, sc)]
        p = [jnp.exp(a - mm) for a, mm in zip(s, m)]
        pc = [jnp.exp(b - mm) for b, mm in zip(sc, m)]
        denom = [jnp.sum(a, axis=-1, keepdims=True) + jnp.sum(b, axis=-1, keepdims=True) for a, b in zip(p, pc)]
        o = [(jnp.dot(a.astype(BF16), vw[:, sl], preferred_element_type=F32)
              + jnp.dot(b.astype(BF16), vc[:, sl], preferred_element_type=F32)) / dd
             for a, b, sl, dd in zip(p, pc, blk, denom)]
        outs += [jnp.where(first_head, o[i], o[i + 1]) for i in range(0, NA_HEAD_GROUP, 2)]
    o_ref[0] = jnp.concatenate(outs, axis=1).astype(BF16)


def _neighborhood_attention(q, k, v, kc, vc, bias):
    b, t, w = q.shape

    def bias_map(bi, r):
        return (_na_first_row(r) - r + NA_KH - 1, 0, 0, 0)

    return pl.pallas_call(
        _na_kernel, grid=(b, GRID_ROWS),
        in_specs=[pl.BlockSpec((1, GRID_W, w), lambda bi, r: (bi, r, 0)),
                  pl.BlockSpec((1, t, w), lambda bi, r: (bi, 0, 0)),
                  pl.BlockSpec((1, t, w), lambda bi, r: (bi, 0, 0)),
                  pl.BlockSpec((1, CTX_LEN, w), lambda bi, r: (bi, 0, 0)),
                  pl.BlockSpec((1, CTX_LEN, w), lambda bi, r: (bi, 0, 0)),
                  pl.BlockSpec((1, NA_HEADS, GRID_W, NA_KH * GRID_W), bias_map)],
        out_specs=pl.BlockSpec((1, GRID_W, w), lambda bi, r: (bi, r, 0)),
        out_shape=jax.ShapeDtypeStruct((b, t, w), BF16),
        compiler_params=_params(("parallel", "parallel"), VMEM_LIMIT), name="neighborhood_attention",
    )(q, k, v, kc, vc, bias)


def _rope_tables(t_len):
    t = jnp.arange(t_len)
    row = (t // GRID_W).astype(F32)
    col = (t % GRID_W).astype(F32)
    n_axis = DN_HEAD_DIM // 4
    freqs = ROPE_THETA ** (-jnp.arange(n_axis, dtype=F32) / n_axis)
    ang = jnp.concatenate([row[:, None] * freqs, col[:, None] * freqs], axis=-1)
    cos = jnp.tile(jnp.cos(ang), (1, 4))
    sin = jnp.sin(ang)
    sin_signed = jnp.tile(jnp.concatenate([-sin, sin], axis=-1), (1, 2))
    return cos, sin_signed


def _conv_silu(x, w):
    tx = x.shape[0]
    t = lax.broadcasted_iota(jnp.int32, x.shape, 0)
    acc = x * w[DN_CONV // 2:DN_CONV // 2 + 1, :]
    for tap in range(DN_CONV):
        shift = DN_CONV // 2 - tap
        if shift == 0:
            continue
        xs = pltpu.roll(x, shift % tx, axis=0)
        valid = (t >= shift) & (t < tx + shift)
        acc = acc + jnp.where(valid, xs, 0.0) * w[tap:tap + 1, :]
    return acc * jax.nn.sigmoid(acc)


def _dnprep_kernel(cq_ref, ck_ref, cv_ref, lq_ref, lk_ref, lv_ref, wq_ref, wk_ref, wv_ref, cos_ref, sin_ref,
                   vk_ref, qp_ref, kt_ref):
    li = lax.broadcasted_iota(jnp.int32, (LANES, LANES), 0) // DN_HEAD_DIM
    lj = lax.broadcasted_iota(jnp.int32, (LANES, LANES), 1) // DN_HEAD_DIM
    same_head = (li == lj).astype(F32)
    half = DN_HEAD_DIM // 2

    def l2norm(y):
        return y * lax.rsqrt(_dot_split_lhs(y * y, same_head.astype(BF16)) + NORM_EPS)

    def rope(y):
        lane = lax.broadcasted_iota(jnp.int32, y.shape, 1)
        partner = jnp.where(lane % DN_HEAD_DIM < half,
                            pltpu.roll(y, LANES - half, axis=1), pltpu.roll(y, half, axis=1))
        return y * cos_ref[...] + partner * sin_ref[...]

    off = 0
    for q_ref, k_ref, v_ref, use_rope in ((cq_ref, ck_ref, cv_ref, False), (lq_ref, lk_ref, lv_ref, True)):
        q = l2norm(_conv_silu(q_ref[0], wq_ref[...]))
        k = l2norm(_conv_silu(k_ref[0], wk_ref[...]))
        v = _conv_silu(v_ref[0], wv_ref[...])
        if use_rope:
            q = rope(q)
            k = rope(k)
        q = q * DN_HEAD_DIM ** -0.5
        tx = q.shape[0]
        rows = slice(off, off + tx)
        left = lax.broadcasted_iota(jnp.int32, q.shape, 1) < DN_HEAD_DIM
        k_sw = pltpu.roll(k, DN_HEAD_DIM, axis=1)
        q_sw = pltpu.roll(q, DN_HEAD_DIM, axis=1)
        vk_ref[0, 0, rows, :] = jnp.where(left, v, k_sw)
        vk_ref[0, 1, rows, :] = jnp.where(left, k_sw, v)
        qp_ref[0, 0, rows, :] = jnp.where(left, 0.0, q_sw)
        qp_ref[0, 1, rows, :] = jnp.where(left, q_sw, 0.0)
        kt_ref[0, 0, :, rows] = k_sw.T
        off += tx


def _dn_prepare(z_c, z_l, conv_w, cos, sin):
    b = z_l.shape[0]
    n_hp = DN_PAIRS

    def part(tx, off):
        return pl.BlockSpec((1, tx, LANES), lambda bi, hp: (bi, 0, off + hp))

    def wpart(off):
        return pl.BlockSpec((DN_CONV, LANES), lambda bi, hp: (0, off + hp))

    tbl = pl.BlockSpec((SEQ, LANES), lambda bi, hp: (0, 0))
    per_head = pl.BlockSpec((1, 2, DN_T, LANES), lambda bi, hp: (bi, hp, 0, 0))
    return pl.pallas_call(
        _dnprep_kernel, grid=(b, n_hp),
        in_specs=[part(CTX_LEN, 0), part(CTX_LEN, n_hp), part(CTX_LEN, 2 * n_hp),
                  part(SEQ, 0), part(SEQ, n_hp), part(SEQ, 2 * n_hp),
                  wpart(0), wpart(n_hp), wpart(2 * n_hp), tbl, tbl],
        out_specs=[per_head, per_head, pl.BlockSpec((1, 1, LANES, DN_T), lambda bi, hp: (bi, hp, 0, 0))],
        out_shape=[jax.ShapeDtypeStruct((b, DN_HEADS, DN_T, LANES), F32),
                   jax.ShapeDtypeStruct((b, DN_HEADS, DN_T, LANES), F32),
                   jax.ShapeDtypeStruct((b, n_hp, LANES, DN_T), F32)],
        compiler_params=_params(("parallel", "parallel"), VMEM_LIMIT), name="dn_prepare",
    )(z_c, z_c, z_c, z_l, z_l, z_l, conv_w, conv_w, conv_w, cos, sin)


def _gate_select_matrices():
    sel = np.zeros((DN_PAIRS, LANES, LANES), np.float32)
    for hp in range(DN_PAIRS):
        for d in range(2):
            for hh in range(2):
                src = d * DN_HEADS + 2 * hp + hh
                sel[hp, src, 2 * d + hh] = 1.0
                sel[hp, 2 * DN_HEADS + src, 4 + 2 * d + hh] = 1.0
    return jnp.asarray(sel)


def _softplus(x):
    return jnp.maximum(x, 0.0) + jnp.log1p(jnp.exp(-jnp.abs(x)))


def _gates_kernel(zc_ref, zl_ref, nega_ref, dtb_ref, sel_ref, gc_ref, gr_ref):
    z = jnp.concatenate([zc_ref[0], zl_ref[0]], axis=0)
    lane = lax.broadcasted_iota(jnp.int32, (DN_CHUNK, LANES), 1)
    ri = lax.broadcasted_iota(jnp.int32, (DN_CHUNK, DN_CHUNK), 0)
    ci = lax.broadcasted_iota(jnp.int32, (DN_CHUNK, DN_CHUNK), 1)
    lower = (ri >= ci).astype(F32)
    upper = (ri <= ci).astype(F32)
    beta = jax.nn.sigmoid(z)
    g = nega_ref[...] * _softplus(z + dtb_ref[...])
    tiles = []
    for c in range(DN_NCHUNK):
        rows = slice(c * DN_CHUNK, (c + 1) * DN_CHUNK)
        gc = g[rows]
        prefix = jnp.dot(lower, gc, precision=HIGHEST, preferred_element_type=F32)
        suffix = jnp.dot(upper, gc, precision=HIGHEST, preferred_element_type=F32)
        tiles.append(jnp.where(lane < 2 * DN_HEADS, beta[rows], jnp.where(lane < 3 * DN_HEADS, prefix, suffix)))
    tile = jnp.concatenate(tiles, axis=0)
    for hp in range(DN_PAIRS):
        cols = jnp.dot(tile, sel_ref[hp], precision=HIGHEST, preferred_element_type=F32)
        gc_ref[0, hp] = cols
        gr_ref[0, hp] = cols.T[0:8, :]


def _dn_gates(z_c, z_l, neg_a, dt_bias, sel):
    b = z_l.shape[0]
    return pl.pallas_call(
        _gates_kernel, grid=(b,),
        in_specs=[pl.BlockSpec((1, CTX_LEN, LANES), lambda bi: (bi, 0, 0)),
                  pl.BlockSpec((1, SEQ, LANES), lambda bi: (bi, 0, 0)),
                  pl.BlockSpec((1, LANES), lambda bi: (0, 0)),
                  pl.BlockSpec((1, LANES), lambda bi: (0, 0)),
                  pl.BlockSpec(sel.shape, lambda bi: (0, 0, 0))],
        out_specs=[pl.BlockSpec((1, DN_PAIRS, DN_T, LANES), lambda bi: (bi, 0, 0, 0)),
                   pl.BlockSpec((1, DN_PAIRS, 8, DN_T), lambda bi: (bi, 0, 0, 0))],
        out_shape=[jax.ShapeDtypeStruct((b, DN_PAIRS, DN_T, LANES), F32),
                   jax.ShapeDtypeStruct((b, DN_PAIRS, 8, DN_T), F32)],
        compiler_params=_params(("parallel",), VMEM_LIMIT), name="dn_gates",
    )(z_c, z_l, neg_a, dt_bias, sel)


def _dot_bf16(a, b):
    return jnp.dot(a.astype(BF16), b.astype(BF16), preferred_element_type=F32)


def _block_masks():
    ri = lax.broadcasted_iota(jnp.int32, (LANES, LANES), 0)
    ci = lax.broadcasted_iota(jnp.int32, (LANES, LANES), 1)
    same = (ri // DN_CHUNK) == (ci // DN_CHUNK)
    rin = ri % DN_CHUNK
    cin = ci % DN_CHUNK
    incl = (same & (rin >= cin), same & (rin <= cin))
    strict = (same & (rin > cin), same & (rin < cin))
    return same, incl, strict


def _scan_phase_a(p, vk_ref, qp_ref, kt_ref, gc_ref, gr_ref, lws_ref, lfin_ref, gl_ref):
    c = DN_CHUNK
    same, incl, strict = _block_masks()
    other = jnp.logical_not(same)
    rows = pl.ds(pl.multiple_of(p * LANES, LANES), LANES)
    vk_a = vk_ref[0, 0, rows, :]
    vk_b = vk_ref[0, 1, rows, :]
    qp_a = qp_ref[0, 0, rows, :]
    qp_b = qp_ref[0, 1, rows, :]
    kts = kt_ref[0, 0, :, rows]
    ktr = pltpu.roll(kts, c, axis=1)
    gct = gc_ref[0, 0, rows, :]
    grt = gr_ref[0, 0, :, rows]
    grr = pltpu.roll(grt, c, axis=1)
    lane8 = lax.broadcasted_iota(jnp.int32, (8, LANES), 1)
    left8 = lane8 < c

    items = []
    for par in range(2):
        hs = slice(par * c, (par + 1) * c)
        vks = jnp.concatenate([vk_a[hs], vk_b[hs]], axis=0)
        qq = jnp.concatenate([qp_a[hs], qp_b[hs]], axis=0)
        kk_in = jnp.where(other, vks, 0.0)
        kk = lax.dot_general(jnp.concatenate([kk_in, qq], axis=0).astype(BF16), kk_in.astype(BF16),
                             (((1,), (1,)), ((), ())), preferred_element_type=F32)
        kt_raw = (jnp.concatenate([ktr[:c], kts[c:]], axis=0) if par == 0
                  else jnp.concatenate([kts[:c], ktr[c:]], axis=0))
        for d in range(2):
            bcol = lambda col: jnp.broadcast_to(gct[hs, col:col + 1], (c, LANES))
            beta = jnp.concatenate([bcol(2 * d), bcol(2 * d + 1)], axis=0)
            g_rows = jnp.concatenate([bcol(4 + 2 * d), bcol(5 + 2 * d)], axis=0)
            if par == 0:
                g_lane = jnp.where(left8, grt[4 + 2 * d:5 + 2 * d], grr[5 + 2 * d:6 + 2 * d])
            else:
                g_lane = jnp.where(left8, grr[4 + 2 * d:5 + 2 * d], grt[5 + 2 * d:6 + 2 * d])
            last = par * c + (c - 1 if d == 0 else 0)
            gl_a = jnp.broadcast_to(grt[4 + 2 * d:5 + 2 * d, last:last + 1], (8, LANES))
            gl_b = jnp.broadcast_to(grt[5 + 2 * d:6 + 2 * d, last:last + 1], (8, LANES))
            g_last = jnp.where(left8, gl_a, gl_b)
            g_cols = jnp.broadcast_to(g_lane[0:1], (LANES, LANES))
            decay = jnp.where(incl[d], jnp.exp(jnp.where(incl[d], g_rows - g_cols, 0.0)), 0.0)
            a = jnp.where(strict[d], kk[:LANES] * decay, 0.0) * beta
            qk = kk[LANES:] * decay
            eg = jnp.exp(g_rows)
            x = vks * beta * jnp.where(other, eg, 1.0)
            qe = qq * eg
            kt_s = jnp.where(other, kt_raw * jnp.exp(jnp.broadcast_to(g_last[0:1] - g_lane[0:1], (LANES, LANES))),
                             0.0)
            idx = (2 * p + par) * 2 + d
            lws_ref[idx, LANES:, :] = qe.astype(BF16)
            lfin_ref[idx] = jnp.concatenate([qk, kt_s], axis=0).astype(BF16)
            gl_ref[idx] = jnp.exp(g_last)
            items.append(dict(idx=idx, d=d, a=a, x=x))
    return items


def _scan_solve(items, xs_ref, lws_ref):
    ri = lax.broadcasted_iota(jnp.int32, (LANES, LANES), 0)
    ci = lax.broadcasted_iota(jnp.int32, (LANES, LANES), 1)
    eye = (ri == ci).astype(F32)
    base = DN_INV_BASE

    def off_block(size, d):
        row_second = (ri // size) % 2 == 1
        col_second = (ci // size) % 2 == 1
        inside = (ri // (2 * size)) == (ci // (2 * size))
        return inside & ((row_second & ~col_second) if d == 0 else (~row_second & col_second))

    for it in items:
        it["a8"] = jnp.where((ri // base) == (ci // base), it["a"], 0.0)
        it["t"] = eye - it["a8"]
    for it in items:
        it["p"] = _dot_bf16(it["a8"], it["a8"])
    for it in items:
        r = _dot_bf16(it["p"], jnp.concatenate([it["p"], it["t"]], axis=1))
        it["p"] = r[:, :LANES]
        it["t"] = it["t"] + r[:, LANES:]
    for it in items:
        it["t"] = it["t"] + _dot_bf16(it["p"], it["t"])
    size = base
    while size < DN_CHUNK:
        for it in items:
            it["m"] = _dot_bf16(jnp.where(off_block(size, it["d"]), it["a"], 0.0), it["t"])
        for it in items:
            it["t"] = it["t"] - _dot_bf16(it["t"], it["m"])
        size *= 2
    for it in items:
        it["x"] = _dot_bf16(it["t"], it["x"])
    for it in items:
        xs_ref[it["idx"]] = it["x"]
        lws_ref[it["idx"], :LANES, :] = it["x"].astype(BF16)


def _scan_phase_b(lo, hi, states, xs_ref, lws_ref, lfin_ref, gl_ref, out_refs):
    c = DN_CHUNK
    same, _, _ = _block_masks()
    other = jnp.logical_not(same)
    left = lax.broadcasted_iota(jnp.int32, (c, LANES), 1) < c

    def body(s, carry):
        u_f = s
        u_b = jnp.where(s < DN_CTX_CHUNKS, DN_CTX_CHUNKS - 1 - s, DN_NCHUNK + DN_CTX_CHUNKS - 1 - s)
        us = (u_f, u_b)
        idx = [us[d] * 2 + d for d in range(2)]
        res = [jnp.dot(lws_ref[idx[d]], carry[d].astype(BF16), preferred_element_type=F32) for d in range(2)]
        vn = [(xs_ref[idx[d]] - res[d][:LANES]).astype(BF16) for d in range(2)]
        fin = [jnp.dot(lfin_ref[idx[d]], vn[d], preferred_element_type=F32) for d in range(2)]
        new = []
        for d in range(2):
            if out_refs is not None:
                o = jnp.where(left, fin[d][:c] + res[d][LANES:LANES + c],
                              fin[d][c:LANES] + res[d][LANES + c:])
                out_refs[d][0, pl.ds(pl.multiple_of((us[d] - DN_CTX_CHUNKS) * c, c), c), :] = o
            decay = jnp.broadcast_to(gl_ref[idx[d]][0:1], (LANES, LANES))
            new.append(carry[d] * decay + jnp.where(other, fin[d][LANES:], 0.0))
        return tuple(new)

    return lax.fori_loop(lo, hi, body, states)


def _scan_kernel(vk_ref, qp_ref, kt_ref, gc_ref, gr_ref, of_ref, ob_ref, xs_ref, lws_ref, lfin_ref, gl_ref):
    def phase_a(step, carry):
        items = []
        for j in range(DN_PAIRS_PER_STEP):
            items += _scan_phase_a(step * DN_PAIRS_PER_STEP + j, vk_ref, qp_ref, kt_ref, gc_ref, gr_ref,
                                   lws_ref, lfin_ref, gl_ref)
        _scan_solve(items, xs_ref, lws_ref)
        return carry

    lax.fori_loop(0, DN_T // LANES // DN_PAIRS_PER_STEP, phase_a, 0)
    zero = jnp.zeros((LANES, LANES), F32)
    scratch = (xs_ref, lws_ref, lfin_ref, gl_ref)
    st = _scan_phase_b(0, DN_CTX_CHUNKS, (zero, zero), *scratch, None)
    _scan_phase_b(DN_CTX_CHUNKS, DN_NCHUNK, st, *scratch, (of_ref, ob_ref))


def _dn_scan(vk, qp, kt, g_cols, g_rows):
    b = vk.shape[0]
    n_items = 2 * DN_NCHUNK
    out = pl.BlockSpec((1, SEQ, LANES), lambda bi, hp: (bi, 0, hp))
    return pl.pallas_call(
        _scan_kernel, grid=(b, DN_PAIRS),
        in_specs=[pl.BlockSpec((1, 2, DN_T, LANES), lambda bi, hp: (bi, hp, 0, 0)),
                  pl.BlockSpec((1, 2, DN_T, LANES), lambda bi, hp: (bi, hp, 0, 0)),
                  pl.BlockSpec((1, 1, LANES, DN_T), lambda bi, hp: (bi, hp, 0, 0)),
                  pl.BlockSpec((1, 1, DN_T, LANES), lambda bi, hp: (bi, hp, 0, 0)),
                  pl.BlockSpec((1, 1, 8, DN_T), lambda bi, hp: (bi, hp, 0, 0))],
        out_specs=[out, out],
        out_shape=[jax.ShapeDtypeStruct((b, SEQ, DN_WIDTH), F32)] * 2,
        scratch_shapes=[pltpu.VMEM((n_items, LANES, LANES), F32),
                        pltpu.VMEM((n_items, 2 * LANES, LANES), BF16),
                        pltpu.VMEM((n_items, 2 * LANES, LANES), BF16),
                        pltpu.VMEM((n_items, 8, LANES), F32)],
        compiler_params=_params(("parallel", "parallel"), VMEM_LIMIT), name="dn_scan",
    )(vk, qp, kt, g_cols, g_rows)


POST_TM = 512
ROW_TILE = 8


def _store_row_tiles(ref, val):
    m = val.shape[0]
    for j in range(ROW_TILE):
        ref[pl.ds(j, m, stride=ROW_TILE), :] = val[:, j * LANES:(j + 1) * LANES]


def _load_row_tiles(ref, first, m):
    return jnp.concatenate([ref[pl.ds(first * ROW_TILE + j, m, stride=ROW_TILE), :] for j in range(ROW_TILE)],
                           axis=1)


def _post_kernel(of_ref, ob_ref, dg_ref, ona_ref, gab_ref, x_ref, g1_ref, sh2_ref, sc2_ref,
                 dnw_ref, wa_ref, wb_ref, wo_ref, n2w_ref, wr_ref, br_ref,
                 xl_ref, h2_ref, route_ref, cnt_ref):
    i = pl.program_id(0)

    @pl.when(i == 0)
    def _():
        cnt_ref[...] = jnp.zeros_like(cnt_ref)

    o = of_ref[...] + ob_ref[...]
    hi = lax.broadcasted_iota(jnp.int32, (DN_WIDTH, DN_WIDTH), 0) // DN_HEAD_DIM
    hj = lax.broadcasted_iota(jnp.int32, (DN_WIDTH, DN_WIDTH), 1) // DN_HEAD_DIM
    head_mean = jnp.where(hi == hj, 1.0 / DN_HEAD_DIM, 0.0).astype(BF16)
    ms = _dot_split_lhs(o * o, head_mean)
    dg = dg_ref[...]
    o_dn = (o * lax.rsqrt(ms + NORM_EPS) * dnw_ref[...]) * (dg * jax.nn.sigmoid(dg))
    gab = gab_ref[...]
    ya = jnp.dot(ona_ref[...], wa_ref[...], preferred_element_type=F32)
    yb = jnp.dot(o_dn.astype(BF16), wb_ref[...], preferred_element_type=F32)
    y = jax.nn.sigmoid(gab[:, :D_MODEL]) * ya + jax.nn.sigmoid(gab[:, D_MODEL:]) * yb
    y = jnp.dot(y.astype(BF16), wo_ref[...], preferred_element_type=F32)
    xl = x_ref[...] + g1_ref[0] * y
    xl_ref[...] = xl
    ms2 = jnp.mean(xl * xl, axis=-1, keepdims=True)
    h2 = (xl * lax.rsqrt(ms2 + NORM_EPS) * n2w_ref[...]) * (1.0 + sc2_ref[0]) + sh2_ref[0]
    _store_row_tiles(h2_ref, h2)

    h_hi, h_lo = _split_bf16(h2)
    logits = (jnp.dot(h_hi, wr_ref[0], preferred_element_type=F32) + jnp.dot(h_lo, wr_ref[0], preferred_element_type=F32)
              + jnp.dot(h_hi, wr_ref[1], preferred_element_type=F32) + br_ref[...])
    tm = logits.shape[0]
    lane = lax.broadcasted_iota(jnp.int32, (tm, LANES), 1).astype(F32)
    vals, idxs = [], []
    cur = logits
    for _ in range(TOP_K):
        m = jnp.max(cur, axis=-1, keepdims=True)
        idx = jnp.min(jnp.where(cur == m, lane, float(LANES)), axis=-1, keepdims=True)
        vals.append(m)
        idxs.append(idx)
        cur = jnp.where(lane == idx, -jnp.inf, cur)
    es = [jnp.exp(v - vals[0]) for v in vals]
    den = es[0] + es[1] + es[2] + es[3]
    onehot = jnp.zeros((tm, LANES), F32)
    for idx in idxs:
        onehot = onehot + jnp.where(lane == idx, 1.0, 0.0)
    ti = lax.broadcasted_iota(jnp.int32, (tm, tm), 0)
    tj = lax.broadcasted_iota(jnp.int32, (tm, tm), 1)
    before = jnp.where(ti > tj, 1.0, 0.0).astype(BF16)
    cnt = cnt_ref[...] + jnp.dot(before, onehot.astype(BF16), preferred_element_type=F32)
    route = jnp.zeros((tm, LANES), F32)
    for kk in range(TOP_K):
        rank = jnp.sum(jnp.where(lane == idxs[kk], cnt, 0.0), axis=-1, keepdims=True)
        route = jnp.where(lane == float(kk), es[kk] / den, route)
        route = jnp.where(lane == float(TOP_K + kk), idxs[kk], route)
        route = jnp.where(lane == float(2 * TOP_K + kk), rank, route)
    route_ref[...] = route
    cnt_ref[...] = cnt_ref[...] + jnp.sum(onehot, axis=0, keepdims=True)


def _post_mixer(o_f, o_b, dg, o_na, gab, x, g1, sh2, sc2, dn_norm_w, w_br_a, w_br_b, w_out, norm2_w, wr, br):
    n, d = x.shape
    tm = POST_TM
    per_batch = SEQ // tm
    tok = lambda w: pl.BlockSpec((tm, w), lambda i: (i, 0))
    row = pl.BlockSpec((1, 1, d), lambda i: (i // per_batch, 0, 0))
    const = lambda a: pl.BlockSpec(a.shape, lambda i: (0,) * a.ndim)
    return pl.pallas_call(
        _post_kernel, grid=(n // tm,),
        in_specs=[tok(DN_WIDTH), tok(DN_WIDTH), tok(DN_WIDTH), tok(NA_WIDTH), tok(2 * D_MODEL), tok(d),
                  row, row, row, const(dn_norm_w), const(w_br_a), const(w_br_b), const(w_out),
                  const(norm2_w), const(wr), const(br)],
        out_specs=[tok(d), pl.BlockSpec((tm * ROW_TILE, LANES), lambda i: (i, 0)), tok(LANES),
                   pl.BlockSpec((1, LANES), lambda i: (0, 0))],
        out_shape=[jax.ShapeDtypeStruct((n, d), F32), jax.ShapeDtypeStruct((n * ROW_TILE, LANES), F32),
                   jax.ShapeDtypeStruct((n, LANES), F32), jax.ShapeDtypeStruct((1, LANES), F32)],
        compiler_params=_params(("arbitrary",), VMEM_LIMIT), name="post_mixer_router",
    )(o_f, o_b, dg, o_na, gab, x, g1, sh2, sc2, dn_norm_w, w_br_a, w_br_b, w_out, norm2_w, wr, br)


SLOT_CHUNK = 8192
SLOT_UNROLL = 16


def _slot_map_kernel(lo_ref, hi_ref, dest_ref, tok_ref):
    c = pl.program_id(0)

    @pl.when(c == 0)
    def _():
        def fill_range(r, carry):
            def fill(s, inner):
                tok_ref[s] = 0
                return inner
            return lax.fori_loop(lo_ref[r], hi_ref[r], fill, carry)
        lax.fori_loop(0, N_EXPERTS + 1, fill_range, 0)

    def put(i, carry):
        slots = [dest_ref[0, 0, i * SLOT_UNROLL + j] for j in range(SLOT_UNROLL)]
        first_tok = lax.shift_right_logical(c * SLOT_CHUNK + i * SLOT_UNROLL, TOP_K.bit_length() - 1)
        for j in range(SLOT_UNROLL):
            tok_ref[slots[j]] = first_tok + j // TOP_K
        return carry
    lax.fori_loop(0, SLOT_CHUNK // SLOT_UNROLL, put, 0)


def _slot_map(pad_lo, pad_hi, dest):
    grid_spec = pltpu.PrefetchScalarGridSpec(
        num_scalar_prefetch=2, grid=(N_ASG // SLOT_CHUNK,),
        in_specs=[pl.BlockSpec((1, 1, SLOT_CHUNK), lambda c, lo, hi: (c, 0, 0), memory_space=pltpu.SMEM)],
        out_specs=pl.BlockSpec((MOE_MPAD,), lambda c, lo, hi: (0,), memory_space=pltpu.SMEM))
    return pl.pallas_call(
        _slot_map_kernel, grid_spec=grid_spec,
        out_shape=jax.ShapeDtypeStruct((MOE_MPAD,), jnp.int32),
        compiler_params=_params(("arbitrary",)), name="slot_map",
    )(pad_lo, pad_hi, dest.reshape(N_ASG // SLOT_CHUNK, 1, SLOT_CHUNK))


MOE_LOOKAHEAD = 2
MOE_SLOTS = MOE_LOOKAHEAD + 1
DMA_ISSUE_UNROLL = 8
DMA_WAIT_UNROLL = 32


def _row_copy(src_hbm, dst_vmem, src_row, dst_row, sem):
    src = pl.ds(pl.multiple_of(src_row * ROW_TILE, ROW_TILE), ROW_TILE)
    dst = pl.ds(pl.multiple_of(dst_row * ROW_TILE, ROW_TILE), ROW_TILE)
    return pltpu.make_async_copy(src_hbm.at[src], dst_vmem.at[dst], sem)


def _expert_kernel(blk_e_ref, nact_ref, tok_ref, par_ref, nxt_ref, h_hbm, w1_hbm, b1_ref, w2_hbm, b2_ref, y_ref,
                   xbuf, w1s, w2s, w1b, w2b, sems, wsems):
    i = pl.program_id(0)
    n_active = nact_ref[0]
    bm = MOE_BM

    def issue(blk, slot):
        def body(g, carry):
            for j in range(DMA_ISSUE_UNROLL):
                s = g * DMA_ISSUE_UNROLL + j
                _row_copy(h_hbm, xbuf.at[slot], tok_ref[blk * bm + s], s, sems.at[slot]).start()
            return carry
        lax.fori_loop(0, bm // DMA_ISSUE_UNROLL, body, 0)

    def wait(slot):
        def body(g, carry):
            for j in range(DMA_WAIT_UNROLL):
                _row_copy(h_hbm, xbuf.at[slot], 0, g * DMA_WAIT_UNROLL + j, sems.at[slot]).wait()
            return carry
        lax.fori_loop(0, bm // DMA_WAIT_UNROLL, body, 0)

    def weight_copies(e, p):
        return (pltpu.make_async_copy(w1_hbm.at[e], w1s.at[p], wsems.at[p]),
                pltpu.make_async_copy(w2_hbm.at[e], w2s.at[p], wsems.at[p]))

    @pl.when(i == 0)
    def _():
        for cp in weight_copies(blk_e_ref[0], par_ref[0]):
            cp.start()
        for j in range(MOE_LOOKAHEAD):
            issue(jnp.minimum(j, n_active - 1), j)

    changed = jnp.logical_or(i == 0, blk_e_ref[i] != blk_e_ref[jnp.maximum(i - 1, 0)])

    @pl.when(jnp.logical_and(i < n_active, changed))
    def _():
        p = par_ref[i]
        for cp in weight_copies(blk_e_ref[i], p):
            cp.wait()
        w1b[...] = w1s[p].astype(BF16)
        w2b[...] = w2s[p].astype(BF16)

        @pl.when(nxt_ref[i] >= 0)
        def _():
            for cp in weight_copies(nxt_ref[i], 1 - p):
                cp.start()

    @pl.when(i < n_active)
    def _():
        slot = i % MOE_SLOTS
        wait(slot)
        xb = _load_row_tiles(xbuf.at[slot], 0, bm).astype(BF16)
        nxt = jnp.minimum(i + MOE_LOOKAHEAD, n_active - 1)
        nslot = (i + MOE_LOOKAHEAD) % MOE_SLOTS
        for s in range(bm):
            _row_copy(h_hbm, xbuf.at[nslot], tok_ref[nxt * bm + s], s, sems.at[nslot]).start()
        hb = jnp.dot(xb, w1b[...], preferred_element_type=F32) + b1_ref[0]
        gate = jnp.minimum(hb[:, :D_EXPERT], SWIGLU_LIMIT)
        up = jnp.clip(hb[:, D_EXPERT:], -SWIGLU_LIMIT, SWIGLU_LIMIT)
        act = (up + 1.0) * gate * jax.nn.sigmoid(SWIGLU_ALPHA * gate)
        _store_row_tiles(y_ref, jnp.dot(act.astype(BF16), w2b[...], preferred_element_type=F32) + b2_ref[0])

    @pl.when(i == n_active)
    def _():
        for j in range(MOE_LOOKAHEAD):
            wait((i + j) % MOE_SLOTS)

    @pl.when(i >= n_active)
    def _():
        y_ref[...] = jnp.zeros_like(y_ref)


def _expert_ffn(blk_expert, n_active, tok_pad, stage_slot, next_expert, h2, w1, b1, w2, b2):
    d = w1.shape[1]
    f2 = w1.shape[2]

    def live(i, nact):
        return jnp.minimum(i, jnp.maximum(nact[0] - 1, 0))

    bias = lambda w: pl.BlockSpec((1, 1, w), lambda i, be, na, tk, sp, ne: (be[live(i, na)], 0, 0))
    grid_spec = pltpu.PrefetchScalarGridSpec(
        num_scalar_prefetch=5, grid=(MOE_NBLK,),
        in_specs=[pl.BlockSpec(memory_space=pl.ANY), pl.BlockSpec(memory_space=pl.ANY), bias(f2),
                  pl.BlockSpec(memory_space=pl.ANY), bias(d)],
        out_specs=pl.BlockSpec((MOE_BM * ROW_TILE, LANES), lambda i, be, na, tk, sp, ne: (i, 0)),
        scratch_shapes=[pltpu.VMEM((MOE_SLOTS, MOE_BM * ROW_TILE, LANES), F32),
                        pltpu.VMEM((2, d, f2), F32), pltpu.VMEM((2, f2 // 2, d), F32),
                        pltpu.VMEM((d, f2), BF16), pltpu.VMEM((f2 // 2, d), BF16),
                        pltpu.SemaphoreType.DMA((MOE_SLOTS,)), pltpu.SemaphoreType.DMA((2,))])
    return pl.pallas_call(
        _expert_kernel, grid_spec=grid_spec,
        out_shape=jax.ShapeDtypeStruct((MOE_MPAD * ROW_TILE, LANES), F32),
        compiler_params=_params(("arbitrary",), VMEM_LIMIT), name="expert_ffn",
    )(blk_expert, n_active, tok_pad, stage_slot, next_expert, h2, w1, b1, w2, b2)


COMB_TM = 128


def _combine_kernel(dest_ref, y_hbm, xl_ref, route_ref, g2_ref, fw_ref, o_ref, ybuf, sems):
    i = pl.program_id(0)
    n_steps = pl.num_programs(0)
    tm = COMB_TM

    def issue(step, slot):
        def body(g, carry):
            for j in range(DMA_ISSUE_UNROLL // TOP_K):
                t = g * (DMA_ISSUE_UNROLL // TOP_K) + j
                for kk in range(TOP_K):
                    _row_copy(y_hbm, ybuf.at[slot], dest_ref[(step * tm + t) * TOP_K + kk], kk * tm + t,
                              sems.at[slot]).start()
            return carry
        lax.fori_loop(0, tm * TOP_K // DMA_ISSUE_UNROLL, body, 0)

    def wait(slot):
        def body(g, carry):
            for j in range(DMA_WAIT_UNROLL):
                _row_copy(y_hbm, ybuf.at[slot], 0, g * DMA_WAIT_UNROLL + j, sems.at[slot]).wait()
            return carry
        lax.fori_loop(0, tm * TOP_K // DMA_WAIT_UNROLL, body, 0)

    @pl.when(i == 0)
    def _():
        issue(0, 0)

    @pl.when(i + 1 < n_steps)
    def _():
        issue(i + 1, (i + 1) % 2)

    slot = i % 2
    wait(slot)
    route = route_ref[...]
    moe = jnp.zeros((tm, D_MODEL), F32)
    for kk in range(TOP_K):
        moe = moe + route[:, kk:kk + 1] * _load_row_tiles(ybuf.at[slot], kk * tm, tm)
    xo = xl_ref[...] + g2_ref[0] * moe
    ms = jnp.mean(xo * xo, axis=-1, keepdims=True)
    o_ref[...] = xo * lax.rsqrt(ms + NORM_EPS) * fw_ref[...]


def _combine(dest, y_sorted, xl, route, g2, final_w):
    n, d = xl.shape
    tm = COMB_TM
    per_batch = SEQ // tm
    grid_spec = pltpu.PrefetchScalarGridSpec(
        num_scalar_prefetch=1, grid=(n // tm,),
        in_specs=[pl.BlockSpec(memory_space=pl.ANY),
                  pl.BlockSpec((tm, d), lambda i, ds: (i, 0)),
                  pl.BlockSpec((tm, LANES), lambda i, ds: (i, 0)),
                  pl.BlockSpec((1, 1, d), lambda i, ds: (i // per_batch, 0, 0)),
                  pl.BlockSpec((1, d), lambda i, ds: (0, 0))],
        out_specs=pl.BlockSpec((tm, d), lambda i, ds: (i, 0)),
        scratch_shapes=[pltpu.VMEM((2, TOP_K * tm * ROW_TILE, LANES), F32), pltpu.SemaphoreType.DMA((2,))])
    return pl.pallas_call(
        _combine_kernel, grid_spec=grid_spec,
        out_shape=jax.ShapeDtypeStruct((n, d), F32),
        compiler_params=_params(("arbitrary",), VMEM_LIMIT), name="moe_combine",
    )(dest, y_sorted, xl, route, g2, final_w)


def _split_in_weights(w_in):
    o = np.cumsum((0, NA_WIDTH, NA_WIDTH, NA_WIDTH, 3 * DN_WIDTH, DN_WIDTH, 2 * DN_HEADS, 2 * DN_HEADS,
                   D_MODEL, D_MODEL))
    wb = w_in.astype(BF16)
    w_na = wb[:, o[0]:o[3]]
    w_dq = wb[:, o[3]:o[4]]
    w_dg = wb[:, o[4]:o[5]]
    w_gab = wb[:, o[7]:o[9]]
    w_dba = jnp.pad(wb[:, o[5]:o[7]], ((0, 0), (0, LANES - 4 * DN_HEADS)))
    return w_na, w_dq, w_dg, w_gab, w_dba


def kernel(x, c, ctx, c_ctx, w_mod, b_mod, norm1_w, w_in, na_rpb, dn_conv_w, dn_a_log, dn_dt_bias, dn_norm_w,
           w_br_a, w_br_b, w_out, norm2_w, w_router, b_router, w1, b1, w2, b2, final_norm_w):
    d = D_MODEL
    cc = jnp.concatenate([c, c_ctx[None], jnp.zeros((16 - BATCH - 1, d), F32)], axis=0)
    mod = _modulation(cc, w_mod[0], b_mod[0][None])
    mod_l = mod[:BATCH].reshape(BATCH, 6, 1, d)
    sh1, sc1, g1, sh2, sc2, g2 = (mod_l[:, i] for i in range(6))
    mod_c = jnp.broadcast_to(mod[BATCH].reshape(6, 1, 1, d), (6, BATCH, 1, d))

    weights = _split_in_weights(w_in[0])
    n1w = norm1_w[0][None]
    naq, nak, nav, dqkv_l, dg_l, gab_l, dba_l = _in_projection(x, sh1, sc1, n1w, weights, 512)
    _, nak_c, nav_c, dqkv_c, _, _, dba_c = _in_projection(ctx, mod_c[0], mod_c[1], n1w, weights, CTX_LEN)

    o_na = _neighborhood_attention(naq, nak, nav, nak_c, nav_c, _na_bias_table(na_rpb[0]))

    cos, sin = _rope_tables(SEQ)
    vk, qp, kt = _dn_prepare(dqkv_c, dqkv_l, dn_conv_w[0], cos, sin)
    pad16 = lambda v: jnp.pad(v.reshape(1, 2 * DN_HEADS), ((0, 0), (2 * DN_HEADS, LANES - 4 * DN_HEADS)))
    g_cols, g_rows = _dn_gates(dba_c, dba_l, pad16(-jnp.exp(dn_a_log[0])), pad16(dn_dt_bias[0]),
                               _gate_select_matrices())
    o_f, o_b = _dn_scan(vk, qp, kt, g_cols, g_rows)

    flat = lambda a: a.reshape(N_TOK, a.shape[-1])
    wr = jnp.stack(_split_bf16(jnp.pad(w_router[0], ((0, 0), (0, LANES - N_EXPERTS)))))
    br = jnp.concatenate([b_router[0], jnp.full((LANES - N_EXPERTS,), NEG_INF, F32)])[None]
    dnw = jnp.tile(dn_norm_w[0], DN_HEADS)[None]
    xl, h2, route, counts = _post_mixer(
        flat(o_f), flat(o_b), flat(dg_l), flat(o_na), flat(gab_l), flat(x), g1, sh2, sc2, dnw,
        w_br_a[0].astype(BF16), w_br_b[0].astype(BF16), w_out[0].astype(BF16), norm2_w[0][None], wr, br)

    e_idx = route[:, TOP_K:2 * TOP_K].astype(jnp.int32)
    rank = route[:, 2 * TOP_K:3 * TOP_K].astype(jnp.int32)
    cnt = counts[0, :N_EXPERTS].astype(jnp.int32)
    padded = (cnt + MOE_BM - 1) // MOE_BM * MOE_BM
    pad_ends = jnp.cumsum(padded)
    pad_starts = pad_ends - padded
    dest = (pad_starts[e_idx] + rank).reshape(-1)
    total = jnp.full((1,), MOE_MPAD, jnp.int32)
    tok_pad = _slot_map(jnp.concatenate([pad_starts + cnt, pad_ends[-1:]]), jnp.concatenate([pad_ends, total]), dest)
    blk_first_row = jnp.arange(MOE_NBLK, dtype=jnp.int32) * MOE_BM
    blk_expert = jnp.minimum(jnp.sum((pad_ends[None, :] <= blk_first_row[:, None]).astype(jnp.int32), axis=1),
                             N_EXPERTS - 1)
    n_active = (pad_ends[-1:] // MOE_BM).astype(jnp.int32)

    new_group = jnp.concatenate([jnp.ones((1,), jnp.int32), (blk_expert[1:] != blk_expert[:-1]).astype(jnp.int32)])
    stage_slot = jnp.cumsum(new_group) % 2
    ids = jnp.arange(N_EXPERTS, dtype=jnp.int32)
    later = (ids[None, :] > ids[:, None]) & (cnt[None, :] > 0)
    next_nonempty = jnp.min(jnp.where(later, ids[None, :], N_EXPERTS), axis=1)
    next_expert = jnp.where(next_nonempty < N_EXPERTS, next_nonempty, -1)[blk_expert]

    y_sorted = _expert_ffn(blk_expert, n_active, tok_pad, stage_slot, next_expert, h2, w1[0], b1[0][:, None], w2[0],
                           b2[0][:, None])
    out = _combine(dest, y_sorted, xl, route, g2, final_norm_w[None])
    return out.reshape(BATCH, SEQ, d)
```
